```python
import math
import jax, jax.numpy as jnp
from jax import lax
import numpy as np

D_MODEL = 1024
BATCH = 4
SEQ = 4096
DEPTH = 4
DEC_BATCH = 32
DEC_SEQ = 4
PAST_LEN = 8192
PAGE_SIZE = 128

HEAD_DIM = 64
SSD_HEADS = 8
SSD_INNER = SSD_HEADS * HEAD_DIM
SSD_GROUPS = 2
SSD_STATE = 64
SSD_CONV_W = 4
SSD_CONV_DIM = SSD_INNER + 2 * SSD_GROUPS * SSD_STATE
SSD_CHUNK = 128
ATT_HEADS = 4
ATT_WIDTH = ATT_HEADS * HEAD_DIM
IDX_HEADS = 4
IDX_DIM = 64
TOPK_MAX = 256
Q_BLOCK = 128
ROPE_THETA = 10000.0
CC_GROUPS = 4
CC_WIDTH = CC_GROUPS * HEAD_DIM
CC_CONV_W = 31
MIX_WIDTH = SSD_INNER + ATT_WIDTH + CC_WIDTH
D_FF = 2816
FFN_CONV_W = 3
ALPHA = (2 * DEPTH) ** 0.25
BETA = (8 * DEPTH) ** -0.25
EPS = 1e-5
IN_SIZES = (SSD_INNER, SSD_CONV_DIM, SSD_HEADS,
            ATT_WIDTH, ATT_WIDTH, ATT_WIDTH,
            IDX_HEADS * IDX_DIM, IDX_DIM, IDX_HEADS,
            2 * CC_WIDTH)
D_IN = sum(IN_SIZES)

kernel_name = 'hybrid_ssd_dsa_conformer_decoder_step'


def layer_norm(x, g, b):
    xf = x.astype(jnp.float32)
    mu = jnp.mean(xf, -1, keepdims=True)
    var = jnp.mean(jnp.square(xf - mu), -1, keepdims=True)
    return ((xf - mu) * lax.rsqrt(var + EPS) * g.astype(jnp.float32) + b.astype(jnp.float32)).astype(x.dtype)


def rms_norm(x, g):
    xf = x.astype(jnp.float32)
    return (xf * lax.rsqrt(jnp.mean(xf * xf, -1, keepdims=True) + EPS) * g.astype(jnp.float32)).astype(x.dtype)


def rope(x, pos):
    half = x.shape[-1] // 2
    inv = ROPE_THETA ** (-jnp.arange(half, dtype=jnp.float32) / half)
    ang = pos.astype(jnp.float32)[:, None] * inv[None, :]
    cos = jnp.cos(ang)[:, None, :].astype(x.dtype)
    sin = jnp.sin(ang)[:, None, :].astype(x.dtype)
    x1, x2 = x[..., :half], x[..., half:]
    return jnp.concatenate([x1 * cos - x2 * sin, x2 * cos + x1 * sin], axis=-1)


def causal_dwconv(x, prev, w, b):
    xp = jnp.concatenate([prev.astype(x.dtype), x], axis=1)
    y = lax.conv_general_dilated(xp, w[:, None, :].astype(x.dtype), window_strides=(1,), padding='VALID',
                                 dimension_numbers=('NWC', 'WIO', 'NWC'), feature_group_count=x.shape[-1])
    return y + b, xp[:, x.shape[1]:]


def ssd_scan(x, dt, a, bm, cm, h0):
    bsz, length, n_h, _ = x.shape
    q = SSD_CHUNK if length % SSD_CHUNK == 0 else length
    nc = length // q
    rep = n_h // bm.shape[2]
    bh = jnp.repeat(bm, rep, axis=2)
    ch = jnp.repeat(cm, rep, axis=2)
    xdt = x * dt[..., None]
    da = dt * a
    causal = jnp.tril(jnp.ones((q, q), bool))[None, :, :, None]

    def chunks(t):
        return t.reshape(bsz, nc, q, *t.shape[2:]).swapaxes(0, 1)

    def step(h, inp):
        xc, dac, bc, cc = inp
        cum = jnp.cumsum(dac, axis=1)
        seg = cum[:, :, None, :] - cum[:, None, :, :]
        decay = jnp.exp(jnp.where(causal, seg, -jnp.inf))
        cb = jnp.einsum('bthn,bshn->btsh', cc, bc) * decay
        y = jnp.einsum('btsh,bshp->bthp', cb, xc)
        y = y + jnp.einsum('bthn,bhpn->bthp', cc, h) * jnp.exp(cum)[..., None]
        last = cum[:, -1]
        w_s = jnp.exp(last[:, None, :] - cum)
        h = h * jnp.exp(last)[:, :, None, None] + jnp.einsum('bshn,bshp->bhpn', bc * w_s[..., None], xc)
        return h, y

    h, ys = lax.scan(step, h0, (chunks(xdt), chunks(da), chunks(bh), chunks(ch)))
    return ys.swapaxes(0, 1).reshape(x.shape), h


def index_select(qi, wi, ki, q_pos, k_top):
    n_keys = ki.shape[1]
    s = jnp.einsum('bthd,bsd->bths', qi, ki).astype(jnp.float32) * IDX_DIM ** -0.5
    score = jnp.einsum('bths,bth->bts', jax.nn.relu(s), wi.astype(jnp.float32) * IDX_HEADS ** -0.5)
    admissible = jnp.arange(n_keys)[None, None, :] <= q_pos[None, :, None]
    score = jnp.where(admissible, score, -jnp.inf)
    vals, idx = lax.top_k(score, k_top)
    return idx, vals > -jnp.inf


def sparse_attend(q, ks, vs, valid):
    s = jnp.einsum('bthd,btkhd->bthk', q, ks).astype(jnp.float32) * HEAD_DIM ** -0.5
    s = jnp.where(valid[:, :, None, :], s, -jnp.inf)
    p = jax.nn.softmax(s, axis=-1).astype(vs.dtype)
    return jnp.einsum('bthk,btkhd->bthd', p, vs)


def gather_rows(rows, idx):
    return jax.vmap(lambda r, i: r[i])(rows, idx)


def prompt_attend(q, k, v, qi, ki, wi):
    bsz, seq = q.shape[:2]
    nb = seq // Q_BLOCK
    k_top = min(TOPK_MAX, seq // 4)

    def blocks(t):
        return t.reshape(bsz, nb, Q_BLOCK, *t.shape[2:]).swapaxes(0, 1)

    def one_block(args):
        i, qb, qib, wib = args
        q_pos = i * Q_BLOCK + jnp.arange(Q_BLOCK)
        idx, valid = index_select(qib, wib, ki, q_pos, k_top)
        return sparse_attend(qb, gather_rows(k, idx), gather_rows(v, idx), valid)

    out = lax.map(one_block, (jnp.arange(nb), blocks(q), blocks(qi), blocks(wi)))
    return out.swapaxes(0, 1).reshape(q.shape)


def make_sample_attend(cache_k, cache_v, cache_kidx, page_table, layer):
    def attend(q, k, v, qi, ki, wi):
        bsz, t_new = q.shape[:2]
        n_keys = PAST_LEN + t_new
        k_top = min(TOPK_MAX, n_keys // 4)
        ki_past = cache_kidx[page_table, layer].reshape(bsz, PAST_LEN, IDX_DIM)
        ki_all = jnp.concatenate([ki_past.astype(ki.dtype), ki], axis=1)
        idx, valid = index_select(qi, wi, ki_all, PAST_LEN + jnp.arange(t_new), k_top)
        in_past = idx < PAST_LEN
        pi = jnp.minimum(idx, PAST_LEN - 1)
        phys = jnp.take_along_axis(page_table, (pi // PAGE_SIZE).reshape(bsz, -1), axis=1).reshape(idx.shape)
        off = pi % PAGE_SIZE
        ni = jnp.clip(idx - PAST_LEN, 0, t_new - 1)

        def pick(cache, new):
            old = cache[phys, layer, off].astype(new.dtype)
            return jnp.where(in_past[..., None, None], old, gather_rows(new, ni))

        return sparse_attend(q, pick(cache_k, k), pick(cache_v, v), valid)
    return attend


def trunk_layer(x, params, l, pos, ssd_conv_prev, ssm_prev, cc_conv_prev, ffn_conv_prev, attend):
    bsz, t, _ = x.shape
    points = [int(c) for c in np.cumsum(IN_SIZES)[:-1]]
    u = x @ params['w_in'][l]
    z, xbc, dt_raw, q, k, v, qi, ki, wi, glu = jnp.split(u, points, axis=-1)
    xbc, ssd_conv_new = causal_dwconv(xbc, ssd_conv_prev, params['ssd_conv_w'][l], params['ssd_conv_b'][l])
    xbc = jax.nn.silu(xbc).astype(jnp.float32)
    gn = SSD_GROUPS * SSD_STATE
    xs = xbc[..., :SSD_INNER].reshape(bsz, t, SSD_HEADS, HEAD_DIM)
    bm = xbc[..., SSD_INNER:SSD_INNER + gn].reshape(bsz, t, SSD_GROUPS, SSD_STATE)
    cm = xbc[..., SSD_INNER + gn:].reshape(bsz, t, SSD_GROUPS, SSD_STATE)
    dt = jax.nn.softplus(dt_raw.astype(jnp.float32) + params['ssd_dt_bias'][l].astype(jnp.float32))
    a = -jnp.exp(params['ssd_a_log'][l].astype(jnp.float32))
    y_ssd, ssm_new = ssd_scan(xs, dt, a, bm, cm, ssm_prev.astype(jnp.float32))
    y_ssd = (y_ssd + params['ssd_d'][l].astype(jnp.float32)[:, None] * xs).reshape(bsz, t, SSD_INNER)
    y_ssd = rms_norm(y_ssd.astype(x.dtype) * jax.nn.silu(z), params['ssd_norm_g'][l])
    q = rope(q.reshape(bsz, t, ATT_HEADS, HEAD_DIM), pos)
    k = rope(k.reshape(bsz, t, ATT_HEADS, HEAD_DIM), pos)
    v = v.reshape(bsz, t, ATT_HEADS, HEAD_DIM)
    qi = rope(qi.reshape(bsz, t, IDX_HEADS, IDX_DIM), pos)
    ki = rope(ki.reshape(bsz, t, 1, IDX_DIM), pos)[:, :, 0]
    y_att = attend(q, k, v, qi, ki, wi).reshape(bsz, t, ATT_WIDTH)
    gl = glu[..., :CC_WIDTH] * jax.nn.sigmoid(glu[..., CC_WIDTH:])
    c, cc_conv_new = causal_dwconv(gl, cc_conv_prev, params['cc_conv_w'][l], params['cc_conv_b'][l])
    y_cc = jax.nn.silu(layer_norm(c, params['cc_ln_g'][l], params['cc_ln_b'][l]))
    mix = jnp.concatenate([y_ssd, y_att, y_cc], axis=-1) @ params['w_out'][l]
    x = layer_norm(ALPHA * x + mix, params['ln1_g'][l], params['ln1_b'][l])
    up = x @ params['ffn_w_up'][l]
    up, ffn_conv_new = causal_dwconv(up, ffn_conv_prev, params['ffn_conv_w'][l], params['ffn_conv_b'][l])
    f = (jax.nn.silu(up[..., D_FF:]) * up[..., :D_FF]) @ params['ffn_w_down'][l]
    x = layer_norm(ALPHA * x + f, params['ln2_g'][l], params['ln2_b'][l])
    return x, (k, v, ki, ssm_new.astype(x.dtype), ssd_conv_new, cc_conv_new, ffn_conv_new)


def setup_inputs(seed: int = 0) -> dict:
    key = jax.random.key(seed)
    keys = iter(jax.random.split(key, 48))
    f32 = jnp.float32

    def nrm(shape, scale):
        return jax.random.normal(next(keys), shape, f32) * scale

    n_pages = PAST_LEN // PAGE_SIZE
    n_used = DEC_BATCH * n_pages
    n_pool = n_used + (n_used + 3) // 4
    x_prompt = nrm((BATCH, SEQ, D_MODEL), 1.0)
    x_sample = nrm((DEC_BATCH, DEC_SEQ, D_MODEL), 1.0)
    cache_k = nrm((n_pool, DEPTH, PAGE_SIZE, ATT_HEADS, HEAD_DIM), 1.0)
    cache_v = nrm((n_pool, DEPTH, PAGE_SIZE, ATT_HEADS, HEAD_DIM), 1.0)
    cache_kidx = nrm((n_pool, DEPTH, PAGE_SIZE, IDX_DIM), 1.0)
    page_table = jax.random.permutation(next(keys), n_pool)[:n_used].reshape(DEC_BATCH, n_pages).astype(jnp.int32)
    state_ssm = nrm((DEC_BATCH, DEPTH, SSD_HEADS, HEAD_DIM, SSD_STATE), 0.5)
    state_ssd_conv = nrm((DEC_BATCH, DEPTH, SSD_CONV_W - 1, SSD_CONV_DIM), 1.0)
    state_cc_conv = nrm((DEC_BATCH, DEPTH, CC_CONV_W - 1, CC_WIDTH), 0.5)
    state_ffn_conv = nrm((DEC_BATCH, DEPTH, FFN_CONV_W - 1, 2 * D_FF), 1.0)
    dt0 = jnp.exp(jax.random.uniform(next(keys), (DEPTH, SSD_HEADS), f32, math.log(1e-3), math.log(1e-1)))
    ssd_dt_bias = dt0 + jnp.log(-jnp.expm1(-dt0))
    ssd_a_log = jnp.log(jax.random.uniform(next(keys), (DEPTH, SSD_HEADS), f32, 1.0, 16.0))
    return {
        'x_prompt': x_prompt, 'x_sample': x_sample,
        'cache_k': cache_k, 'cache_v': cache_v, 'cache_kidx': cache_kidx, 'page_table': page_table,
        'state_ssm': state_ssm, 'state_ssd_conv': state_ssd_conv,
        'state_cc_conv': state_cc_conv, 'state_ffn_conv': state_ffn_conv,
        'ln0_g': 1.0 + nrm((D_MODEL,), 0.02), 'ln0_b': nrm((D_MODEL,), 0.02),
        'w_in': nrm((DEPTH, D_MODEL, D_IN), D_MODEL ** -0.5),
        'ssd_conv_w': nrm((DEPTH, SSD_CONV_W, SSD_CONV_DIM), SSD_CONV_W ** -0.5),
        'ssd_conv_b': nrm((DEPTH, SSD_CONV_DIM), 0.02),
        'ssd_dt_bias': ssd_dt_bias, 'ssd_a_log': ssd_a_log,
        'ssd_d': 1.0 + nrm((DEPTH, SSD_HEADS), 0.02),
        'ssd_norm_g': 1.0 + nrm((DEPTH, SSD_INNER), 0.02),
        'cc_conv_w': nrm((DEPTH, CC_CONV_W, CC_WIDTH), CC_CONV_W ** -0.5),
        'cc_conv_b': nrm((DEPTH, CC_WIDTH), 0.02),
        'cc_ln_g': 1.0 + nrm((DEPTH, CC_WIDTH), 0.02), 'cc_ln_b': nrm((DEPTH, CC_WIDTH), 0.02),
        'w_out': nrm((DEPTH, MIX_WIDTH, D_MODEL), MIX_WIDTH ** -0.5 * BETA),
        'ln1_g': 1.0 + nrm((DEPTH, D_MODEL), 0.02), 'ln1_b': nrm((DEPTH, D_MODEL), 0.02),
        'ffn_w_up': nrm((DEPTH, D_MODEL, 2 * D_FF), D_MODEL ** -0.5),
        'ffn_conv_w': nrm((DEPTH, FFN_CONV_W, 2 * D_FF), FFN_CONV_W ** -0.5),
        'ffn_conv_b': nrm((DEPTH, 2 * D_FF), 0.02),
        'ffn_w_down': nrm((DEPTH, D_FF, D_MODEL), D_FF ** -0.5 * BETA),
        'ln2_g': 1.0 + nrm((DEPTH, D_MODEL), 0.02), 'ln2_b': nrm((DEPTH, D_MODEL), 0.02),
    }


def reference(x_prompt, x_sample, cache_k, cache_v, cache_kidx, page_table, state_ssm, state_ssd_conv,
              state_cc_conv, state_ffn_conv, ln0_g, ln0_b, w_in, ssd_conv_w, ssd_conv_b, ssd_dt_bias,
              ssd_a_log, ssd_d, ssd_norm_g, cc_conv_w, cc_conv_b, cc_ln_g, cc_ln_b, w_out, ln1_g, ln1_b,
              ffn_w_up, ffn_conv_w, ffn_conv_b, ffn_w_down, ln2_g, ln2_b):
    params = {'w_in': w_in, 'ssd_conv_w': ssd_conv_w, 'ssd_conv_b': ssd_conv_b, 'ssd_dt_bias': ssd_dt_bias,
              'ssd_a_log': ssd_a_log, 'ssd_d': ssd_d, 'ssd_norm_g': ssd_norm_g, 'cc_conv_w': cc_conv_w,
              'cc_conv_b': cc_conv_b, 'cc_ln_g': cc_ln_g, 'cc_ln_b': cc_ln_b, 'w_out': w_out,
              'ln1_g': ln1_g, 'ln1_b': ln1_b, 'ffn_w_up': ffn_w_up, 'ffn_conv_w': ffn_conv_w,
              'ffn_conv_b': ffn_conv_b, 'ffn_w_down': ffn_w_down, 'ln2_g': ln2_g, 'ln2_b': ln2_b}

    bp, sp = x_prompt.shape[:2]
    dtp = x_prompt.dtype
    h = layer_norm(x_prompt, ln0_g, ln0_b)
    pos_p = jnp.arange(sp)
    p_states = []
    for l in range(DEPTH):
        h, st = trunk_layer(h, params, l, pos_p,
                            jnp.zeros((bp, SSD_CONV_W - 1, SSD_CONV_DIM), dtp),
                            jnp.zeros((bp, SSD_HEADS, HEAD_DIM, SSD_STATE), dtp),
                            jnp.zeros((bp, CC_CONV_W - 1, CC_WIDTH), dtp),
                            jnp.zeros((bp, FFN_CONV_W - 1, 2 * D_FF), dtp),
                            prompt_attend)
        p_states.append(st)
    y_prompt = h

    ts = x_sample.shape[1]
    g = layer_norm(x_sample, ln0_g, ln0_b)
    pos_s = PAST_LEN + jnp.arange(ts)
    s_states = []
    for l in range(DEPTH):
        attend = make_sample_attend(cache_k, cache_v, cache_kidx, page_table, l)
        g, st = trunk_layer(g, params, l, pos_s, state_ssd_conv[:, l], state_ssm[:, l],
                            state_cc_conv[:, l], state_ffn_conv[:, l], attend)
        s_states.append(st)
    y_sample = g

    p_k, p_v, p_kidx, p_ssm, p_ssd_conv, p_cc_conv, p_ffn_conv = [jnp.stack(a, axis=1) for a in zip(*p_states)]
    s_k, s_v, s_kidx, s_ssm, s_ssd_conv, s_cc_conv, s_ffn_conv = [jnp.stack(a, axis=1) for a in zip(*s_states)]
    return (y_prompt, y_sample, p_k, p_v, p_kidx, p_ssm, p_ssd_conv, p_cc_conv, p_ffn_conv,
            s_k, s_v, s_kidx, s_ssm, s_ssd_conv, s_cc_conv, s_ffn_conv)
```

```python
import functools
import math

import jax
import jax.numpy as jnp
import numpy as np
from jax import lax
from jax.experimental import pallas as pl
from jax.experimental.pallas import tpu as pltpu

D_MODEL = 1024
DEPTH = 4
PAST_LEN = 8192
PAGE_SIZE = 128
HEAD_DIM = 64
SSD_HEADS = 8
SSD_INNER = SSD_HEADS * HEAD_DIM
SSD_GROUPS = 2
SSD_STATE = 64
SSD_CONV_W = 4
SSD_CONV_DIM = SSD_INNER + 2 * SSD_GROUPS * SSD_STATE
SSD_CHUNK = 128
ATT_HEADS = 4
ATT_WIDTH = ATT_HEADS * HEAD_DIM
IDX_HEADS = 4
IDX_DIM = 64
TOPK_MAX = 256
ROPE_THETA = 10000.0
CC_WIDTH = 256
CC_CONV_W = 31
D_FF = 2816
FFN_CONV_W = 3
ALPHA = (2 * DEPTH) ** 0.25
EPS = 1e-5
IN_SIZES = (SSD_INNER, SSD_CONV_DIM, SSD_HEADS, ATT_WIDTH, ATT_WIDTH, ATT_WIDTH,
            IDX_HEADS * IDX_DIM, IDX_DIM, IDX_HEADS, 2 * CC_WIDTH)

LANES = 128
SUBLANES = 8
VMEM_LIMIT = 56 * 1024 * 1024

C_Z = 0
C_XBC = C_Z + SSD_INNER
C_Q = C_XBC + SSD_CONV_DIM
C_K = C_Q + ATT_WIDTH
C_V = C_K + ATT_WIDTH
C_QI = C_V + ATT_WIDTH
C_GLU = C_QI + IDX_HEADS * IDX_DIM
C_MISC = C_GLU + 2 * CC_WIDTH
C_DT = C_MISC + LANES
C_END = C_DT + LANES
WI_LANE = IDX_DIM

NEG_BIG = -1e30
KEY_NEG_INF = np.int32(-2139095041)
INT32_MIN = np.int32(-2 ** 31)

F32 = jnp.float32
BF16 = jnp.bfloat16
NT_DIMS = (((1,), (1,)), ((), ()))


def _round_up(n, m):
    return (n + m - 1) // m * m


def _silu(x):
    return x * jax.nn.sigmoid(x)


def _layer_norm(x, g, b):
    mu = jnp.mean(x, -1, keepdims=True)
    var = jnp.mean(jnp.square(x - mu), -1, keepdims=True)
    return (x - mu) * lax.rsqrt(var + EPS) * g + b


def _dot(a, b):
    return jnp.dot(a, b, preferred_element_type=F32)


def _dot_nt(a, b):
    return lax.dot_general(a, b, NT_DIMS, preferred_element_type=F32)


def _params(n_axes):
    return pltpu.CompilerParams(dimension_semantics=("arbitrary",) * n_axes, vmem_limit_bytes=VMEM_LIMIT)


def _rope(x, cos, sin_signed):
    n = x.shape[-1]
    lane = lax.broadcasted_iota(jnp.int32, x.shape, 1)
    first_half = (lane % HEAD_DIM) < (HEAD_DIM // 2)
    partner = jnp.where(first_half, pltpu.roll(x, n - HEAD_DIM // 2, 1), pltpu.roll(x, HEAD_DIM // 2, 1))
    return x * cos + partner * sin_signed


def _score_key(score):
    score = jnp.where(score == 0.0, 0.0, score)
    bits = pltpu.bitcast(score, jnp.int32)
    return jnp.where(bits < 0, bits ^ np.int32(0x7FFFFFFF), bits)


def _split3(x):
    hi = x.astype(BF16)
    r1 = x - hi.astype(F32)
    mid = r1.astype(BF16)
    lo = (r1 - mid.astype(F32)).astype(BF16)
    return hi, mid, lo


def _ln_kernel(x_ref, g_ref, b_ref, o_ref):
    o_ref[...] = _layer_norm(x_ref[...], g_ref[...], b_ref[...])


def _ln_rows(x2d, g, b, rows):
    n, d = x2d.shape
    return pl.pallas_call(
        _ln_kernel,
        grid=(n // rows,),
        in_specs=[pl.BlockSpec((rows, d), lambda i: (i, 0)),
                  pl.BlockSpec((1, d), lambda i: (0, 0)),
                  pl.BlockSpec((1, d), lambda i: (0, 0))],
        out_specs=pl.BlockSpec((rows, d), lambda i: (i, 0)),
        out_shape=jax.ShapeDtypeStruct((n, d), F32),
        compiler_params=_params(1),
        name="ln0",
    )(x2d, g.reshape(1, d), b.reshape(1, d))


def _premix_kernel(x_ref, w_ref, scw_ref, scb_ref, ccw_ref, ccb_ref, lng_ref, lnb_ref, dtb_ref,
                   cosq_ref, sinq_ref, cosm_ref, sinm_ref, pssd_ref, pcc_ref,
                   z_ref, xbc_ref, q_ref, k_ref, v_ref, qi_ref, misc_ref, kidx_ref, dt_ref, ycc_ref,
                   nssd_ref, ncc_ref, sbuf, cbuf, *, tt, rs, nt, rb):
    i = pl.program_id(1)
    hs_rows = (SSD_CONV_W - 1) * rs
    hc_rows = (CC_CONV_W - 1) * rs
    hs = _round_up(hs_rows, SUBLANES)
    hc = _round_up(hc_rows, SUBLANES)

    @pl.when(i == 0)
    def _():
        sbuf[hs - hs_rows:hs, :] = pssd_ref[...]
        cbuf[hc - hc_rows:hc, :] = pcc_ref[...]

    xb = x_ref[...].astype(BF16)

    def seg(a, b):
        return _dot(xb, w_ref[:, a:b])

    z_ref[...] = seg(C_Z, C_XBC)

    sbuf[hs:hs + tt, :] = seg(C_XBC, C_Q)
    for r0 in range(0, tt, rb):
        acc = scb_ref[...] + scw_ref[0:1, :] * sbuf[hs - hs_rows + r0:hs - hs_rows + r0 + rb, :]
        for j in range(1, SSD_CONV_W):
            off = hs - (SSD_CONV_W - 1 - j) * rs + r0
            acc = acc + scw_ref[j:j + 1, :] * sbuf[off:off + rb, :]
        xbc_ref[r0:r0 + rb, :] = _silu(acc)

    @pl.when(i == nt - 1)
    def _():
        nssd_ref[...] = sbuf[hs + tt - hs_rows:hs + tt, :]

    sbuf[hs - hs_rows:hs, :] = sbuf[hs + tt - hs_rows:hs + tt, :]

    dtr = seg(C_DT, C_END) + dtb_ref[...]
    sp = jnp.maximum(dtr, 0.0) + jnp.log1p(jnp.exp(-jnp.abs(dtr)))
    lane = lax.broadcasted_iota(jnp.int32, sp.shape, 1)
    dt_ref[...] = jnp.where(lane < SSD_HEADS, sp, 0.0)

    cosq = cosq_ref[...]
    sinq = sinq_ref[...]
    q_ref[...] = _rope(seg(C_Q, C_K), cosq, sinq).astype(q_ref.dtype)
    k_ref[...] = _rope(seg(C_K, C_V), cosq, sinq)
    v_ref[...] = seg(C_V, C_QI)
    qi_ref[...] = _rope(seg(C_QI, C_GLU), cosq, sinq).astype(qi_ref.dtype)
    misc = _rope(seg(C_MISC, C_DT), cosm_ref[...], sinm_ref[...])
    misc_ref[...] = misc
    kidx_ref[...] = misc[:, :IDX_DIM]

    glu = seg(C_GLU, C_MISC)
    cbuf[hc:hc + tt, :] = glu[:, :CC_WIDTH] * jax.nn.sigmoid(glu[:, CC_WIDTH:])
    for r0 in range(0, tt, rb):
        acc = ccb_ref[...] + ccw_ref[0:1, :] * cbuf[hc - hc_rows + r0:hc - hc_rows + r0 + rb, :]
        for j in range(1, CC_CONV_W):
            off = hc - (CC_CONV_W - 1 - j) * rs + r0
            acc = acc + ccw_ref[j:j + 1, :] * cbuf[off:off + rb, :]
        ycc_ref[r0:r0 + rb, :] = _silu(_layer_norm(acc, lng_ref[...], lnb_ref[...]))

    @pl.when(i == nt - 1)
    def _():
        ncc_ref[...] = cbuf[hc + tt - hc_rows:hc + tt, :]

    cbuf[hc - hc_rows:hc, :] = cbuf[hc + tt - hc_rows:hc + tt, :]


def _premix(x, wl, tabs, prev_ssd, prev_cc, *, tt, rs):
    nb, t, _ = x.shape
    nt = t // tt
    rb = min(tt, 64)
    hs = _round_up((SSD_CONV_W - 1) * rs, SUBLANES)
    hc = _round_up((CC_CONV_W - 1) * rs, SUBLANES)

    def tile(c):
        return pl.BlockSpec((None, tt, c), lambda b, i: (b, i, 0))

    def const(shape):
        return pl.BlockSpec(shape, lambda b, i: (0,) * len(shape))

    def tab(c):
        return pl.BlockSpec((tt, c), lambda b, i: (i, 0))

    def state(rows, c):
        return pl.BlockSpec((None, rows, c), lambda b, i: (b, 0, 0))

    widths = (SSD_INNER, SSD_CONV_DIM, ATT_WIDTH, ATT_WIDTH, ATT_WIDTH, IDX_HEADS * IDX_DIM,
              LANES, IDX_DIM, LANES, CC_WIDTH)
    dtypes = (F32, F32, BF16, F32, F32, BF16, F32, F32, F32, F32)
    out_shape = [jax.ShapeDtypeStruct((nb, t, c), d) for c, d in zip(widths, dtypes)]
    out_shape += [jax.ShapeDtypeStruct(prev_ssd.shape, F32), jax.ShapeDtypeStruct(prev_cc.shape, F32)]
    out_specs = [tile(c) for c in widths] + [state(prev_ssd.shape[1], SSD_CONV_DIM),
                                             state(prev_cc.shape[1], CC_WIDTH)]
    return pl.pallas_call(
        functools.partial(_premix_kernel, tt=tt, rs=rs, nt=nt, rb=rb),
        grid=(nb, nt),
        in_specs=[tile(D_MODEL), const((D_MODEL, C_END)),
                  const((SSD_CONV_W, SSD_CONV_DIM)), const((1, SSD_CONV_DIM)),
                  const((CC_CONV_W, CC_WIDTH)), const((1, CC_WIDTH)), const((1, CC_WIDTH)), const((1, CC_WIDTH)),
                  const((1, LANES)),
                  tab(ATT_WIDTH), tab(ATT_WIDTH), tab(LANES), tab(LANES),
                  state(prev_ssd.shape[1], SSD_CONV_DIM), state(prev_cc.shape[1], CC_WIDTH)],
        out_specs=out_specs,
        out_shape=out_shape,
        scratch_shapes=[pltpu.VMEM((hs + tt, SSD_CONV_DIM), F32), pltpu.VMEM((hc + tt, CC_WIDTH), F32)],
        compiler_params=_params(2),
        name="premix",
    )(x, wl["w_in"], wl["ssd_conv_w"], wl["ssd_conv_b"], wl["cc_conv_w"], wl["cc_conv_b"],
      wl["cc_ln_g"], wl["cc_ln_b"], wl["dt_bias"], tabs["cosq"], tabs["sinq"], tabs["cosm"], tabs["sinm"],
      prev_ssd, prev_cc)


def _ssd_kernel(xbc_ref, dt_ref, z_ref, h0_ref, alog_ref, drow_ref, g_ref, y_ref, hout_ref, h_scr, y_scr,
                *, nt, t_valid):
    i = pl.program_id(1)
    q = SSD_CHUNK

    @pl.when(i == 0)
    def _():
        h_scr[...] = h0_ref[...]

    row = lax.broadcasted_iota(jnp.int32, (q, q), 0)
    col = lax.broadcasted_iota(jnp.int32, (q, q), 1)
    causal = col <= row
    tril = jnp.where(causal, 1.0, 0.0).astype(BF16)

    dt = dt_ref[...]
    if t_valid is not None:
        dt = jnp.where(row < t_valid, dt, 0.0)
    lane1 = lax.broadcasted_iota(jnp.int32, (1, LANES), 1)
    a_row = jnp.where(lane1 < SSD_HEADS, -jnp.exp(alog_ref[...]), 0.0)
    da = dt * a_row
    hi, mid, lo = _split3(da)
    cum = _dot(tril, hi) + _dot(tril, mid) + _dot(tril, lo)
    cum_t = cum.T
    last = cum[q - 1:q, :]
    w_s = jnp.exp(last - cum)
    e_cum = jnp.exp(cum)
    e_last = jnp.exp(last)

    erow = lax.broadcasted_iota(jnp.int32, (LANES, SSD_INNER), 0)
    ecol = lax.broadcasted_iota(jnp.int32, (LANES, SSD_INNER), 1)
    expand = jnp.where(ecol // HEAD_DIM == erow, 1.0, 0.0).astype(BF16)
    dhi, dmid, dlo = _split3(dt)
    dt_wide = _dot(dhi, expand) + _dot(dmid, expand) + _dot(dlo, expand)

    xs = xbc_ref[:, :SSD_INNER]
    xdt = xs * dt_wide
    xdt_t = xdt.T
    gn = SSD_GROUPS * SSD_STATE
    rep = SSD_HEADS // SSD_GROUPS
    for g in range(SSD_GROUPS):
        b_g = xbc_ref[:, SSD_INNER + g * SSD_STATE:SSD_INNER + (g + 1) * SSD_STATE]
        c_g = xbc_ref[:, SSD_INNER + gn + g * SSD_STATE:SSD_INNER + gn + (g + 1) * SSD_STATE]
        c_bf = c_g.astype(BF16)
        cb = _dot_nt(c_bf, b_g.astype(BF16))
        for h in range(g * rep, (g + 1) * rep):
            sl = slice(h * HEAD_DIM, (h + 1) * HEAD_DIM)
            seg = cum[:, h:h + 1] - cum_t[h:h + 1, :]
            decay = jnp.exp(jnp.where(causal, seg, -jnp.inf))
            m = (cb * decay).astype(BF16)
            h_prev = h_scr[h]
            y = _dot(m, xdt[:, sl].astype(BF16))
            y = y + _dot_nt(c_bf, h_prev.astype(BF16)) * e_cum[:, h:h + 1]
            bw = (b_g * w_s[:, h:h + 1]).astype(BF16)
            h_scr[h] = h_prev * e_last[:, h:h + 1] + _dot(xdt_t[sl, :].astype(BF16), bw)
            y_scr[:, sl] = y

    y = (y_scr[...] + drow_ref[...] * xs) * _silu(z_ref[...])
    y_ref[...] = y * lax.rsqrt(jnp.mean(y * y, -1, keepdims=True) + EPS) * g_ref[...]

    @pl.when(i == nt - 1)
    def _():
        hout_ref[...] = h_scr[...]


def _ssd(xbc, dt, z, h0, wl, *, t_valid=None):
    nb, t, _ = xbc.shape
    nt = t // SSD_CHUNK

    def tile(c):
        return pl.BlockSpec((None, SSD_CHUNK, c), lambda b, i: (b, i, 0))

    def const(c):
        return pl.BlockSpec((1, c), lambda b, i: (0, 0))

    hspec = pl.BlockSpec((None, SSD_HEADS, HEAD_DIM, SSD_STATE), lambda b, i: (b, 0, 0, 0))
    return pl.pallas_call(
        functools.partial(_ssd_kernel, nt=nt, t_valid=t_valid),
        grid=(nb, nt),
        in_specs=[tile(SSD_CONV_DIM), tile(LANES), tile(SSD_INNER), hspec,
                  const(LANES), const(SSD_INNER), const(SSD_INNER)],
        out_specs=[tile(SSD_INNER), hspec],
        out_shape=[jax.ShapeDtypeStruct((nb, t, SSD_INNER), F32),
                   jax.ShapeDtypeStruct((nb, SSD_HEADS, HEAD_DIM, SSD_STATE), F32)],
        scratch_shapes=[pltpu.VMEM((SSD_HEADS, HEAD_DIM, SSD_STATE), F32),
                        pltpu.VMEM((SSD_CHUNK, SSD_INNER), F32)],
        compiler_params=_params(2),
        name="ssd",
    )(xbc, dt, z, h0, wl["a_log"], wl["ssd_d"], wl["ssd_norm_g"])


def _kth_largest(count_ge, rows, k_top):
    def body(it, prefix):
        cand = prefix ^ lax.shift_left(jnp.int32(1), 31 - it)
        return jnp.where(count_ge(cand) >= k_top, cand, prefix)
    return lax.fori_loop(0, 32, body, jnp.full((rows, 1), INT32_MIN, jnp.int32))


def _upper_tri():
    r = lax.broadcasted_iota(jnp.int32, (LANES, LANES), 0)
    c = lax.broadcasted_iota(jnp.int32, (LANES, LANES), 1)
    return jnp.where(r <= c, 1.0, 0.0).astype(BF16)


def _select_tile(keys, thr, need, carry, utri):
    eq = keys == thr
    eqf = jnp.where(eq, 1.0, 0.0)
    pc = _dot(eqf.astype(BF16), utri) + carry
    sel = (keys > thr) | (eq & (pc <= need) & (thr > KEY_NEG_INF))
    return jnp.where(sel, 0.0, NEG_BIG), carry + jnp.sum(eqf, axis=1, keepdims=True)


def _pattn_kernel(qi_ref, misc_ref, q_ref, kidx_ref, k_ref, v_ref, o_ref, keys_scr, bias_scr,
                  *, tq, ck, k_top):
    i = pl.program_id(1)
    nch = (i * tq + tq + ck - 1) // ck
    sub = ck // LANES
    qi = qi_ref[...]
    wcol = misc_ref[:, WI_LANE:WI_LANE + IDX_HEADS] * IDX_HEADS ** -0.5
    row_pos = i * tq + lax.broadcasted_iota(jnp.int32, (tq, ck), 0)
    col0 = lax.broadcasted_iota(jnp.int32, (tq, ck), 1)

    def score_chunk(c, carry):
        base = pl.multiple_of(c * ck, ck)
        kc = kidx_ref[pl.ds(base, ck), :].astype(BF16)
        sc = jnp.zeros((tq, ck), F32)
        for h in range(IDX_HEADS):
            s = _dot_nt(qi[:, h * IDX_DIM:(h + 1) * IDX_DIM], kc) * IDX_DIM ** -0.5
            sc = sc + jnp.maximum(s, 0.0) * wcol[:, h:h + 1]
        keys_scr[c] = jnp.where(col0 + base <= row_pos, _score_key(sc), KEY_NEG_INF)
        return carry

    lax.fori_loop(0, nch, score_chunk, 0)

    def count(pred):
        def body(c, acc):
            kc = keys_scr[c]
            for j in range(sub):
                acc = acc + jnp.where(pred(kc[:, j * LANES:(j + 1) * LANES]), 1, 0)
            return acc
        acc = lax.fori_loop(0, nch, body, jnp.zeros((tq, LANES), jnp.int32))
        return jnp.sum(acc, axis=1, keepdims=True)

    thr = _kth_largest(lambda cand: count(lambda kk: kk >= cand), tq, k_top)
    need = (k_top - count(lambda kk: kk > thr)).astype(F32)
    utri = _upper_tri()

    def tie_chunk(c, carry):
        kc = keys_scr[c]
        for j in range(sub):
            bias, carry = _select_tile(kc[:, j * LANES:(j + 1) * LANES], thr, need, carry, utri)
            bias_scr[c, :, j * LANES:(j + 1) * LANES] = bias
        return carry

    lax.fori_loop(0, nch, tie_chunk, jnp.zeros((tq, 1), F32))

    q = q_ref[...]
    for h in range(ATT_HEADS):
        sl = slice(h * HEAD_DIM, (h + 1) * HEAD_DIM)
        qh = q[:, sl]

        def att_chunk(c, st):
            m, l, acc = st
            base = pl.multiple_of(c * ck, ck)
            kc = k_ref[pl.ds(base, ck), sl].astype(BF16)
            vc = v_ref[pl.ds(base, ck), sl].astype(BF16)
            s = _dot_nt(qh, kc) * HEAD_DIM ** -0.5 + bias_scr[c]
            m_new = jnp.maximum(m, jnp.max(s, axis=1, keepdims=True))
            alpha = jnp.exp(m - m_new)
            p = jnp.exp(s - m_new)
            l = l * alpha + jnp.sum(p, axis=1, keepdims=True)
            acc = acc * alpha + _dot(p.astype(BF16), vc)
            return m_new, l, acc

        m, l, acc = lax.fori_loop(0, nch, att_chunk,
                                  (jnp.full((tq, 1), NEG_BIG, F32), jnp.zeros((tq, 1), F32),
                                   jnp.zeros((tq, HEAD_DIM), F32)))
        o_ref[:, sl] = acc / l


def _prompt_attend(qi, misc, q, kidx, k, v, *, tq, ck):
    nb, t, _ = q.shape
    k_top = min(TOPK_MAX, t // 4)

    def tile(c):
        return pl.BlockSpec((None, tq, c), lambda b, i: (b, i, 0))

    def full(c):
        return pl.BlockSpec((None, t, c), lambda b, i: (b, 0, 0))

    return pl.pallas_call(
        functools.partial(_pattn_kernel, tq=tq, ck=ck, k_top=k_top),
        grid=(nb, t // tq),
        in_specs=[tile(IDX_HEADS * IDX_DIM), tile(LANES), tile(ATT_WIDTH),
                  full(IDX_DIM), full(ATT_WIDTH), full(ATT_WIDTH)],
        out_specs=tile(ATT_WIDTH),
        out_shape=jax.ShapeDtypeStruct((nb, t, ATT_WIDTH), F32),
        scratch_shapes=[pltpu.VMEM((t // ck, tq, ck), jnp.int32), pltpu.VMEM((t // ck, tq, ck), F32)],
        compiler_params=_params(2),
        name="prompt_attend",
    )(qi, misc, q, kidx, k, v)


TOK_PAD = 8


def _sselect_kernel(pt_ref, qs_ref, misc_ref, knew_ref, *rest, pps, n_pages, t_new, k_top):
    page_refs = rest[:pps]
    bias_ref = rest[pps]
    keys_scr = rest[pps + 1]
    j = pl.program_id(1)
    nsteps = n_pages // pps
    qs = qs_ref[...]
    wcol = misc_ref[:, WI_LANE:WI_LANE + IDX_HEADS] * IDX_HEADS ** -0.5

    def scores(kp):
        s = _dot_nt(qs, kp.astype(BF16)) * IDX_DIM ** -0.5
        sc = jnp.zeros((TOK_PAD, LANES), F32)
        for h in range(IDX_HEADS):
            sc = sc + jnp.maximum(s[h * TOK_PAD:(h + 1) * TOK_PAD], 0.0) * wcol[:, h:h + 1]
        return _score_key(sc)

    for p in range(pps):
        keys_scr[j * pps + p] = scores(page_refs[p][...])

    @pl.when(j == nsteps - 1)
    def _():
        row = lax.broadcasted_iota(jnp.int32, (TOK_PAD, LANES), 0)
        col = lax.broadcasted_iota(jnp.int32, (TOK_PAD, LANES), 1)
        keys_scr[n_pages] = jnp.where((col <= row) & (col < t_new), scores(knew_ref[...]), KEY_NEG_INF)

        def count(pred):
            def body(c, acc):
                return acc + jnp.where(pred(keys_scr[c]), 1, 0)
            acc = lax.fori_loop(0, n_pages + 1, body, jnp.zeros((TOK_PAD, LANES), jnp.int32))
            return jnp.sum(acc, axis=1, keepdims=True)

        thr = _kth_largest(lambda cand: count(lambda kk: kk >= cand), TOK_PAD, k_top)
        need = (k_top - count(lambda kk: kk > thr)).astype(F32)
        utri = _upper_tri()

        def tie_chunk(c, carry):
            bias, carry = _select_tile(keys_scr[c], thr, need, carry, utri)
            bias_ref[c] = bias
            return carry

        lax.fori_loop(0, n_pages + 1, tie_chunk, jnp.zeros((TOK_PAD, 1), F32))


def _sample_select(page_table, qs, misc_bm, kidx_new, cache_kidx, layer, *, pps, t_new):
    nb, n_pages = page_table.shape
    k_top = min(TOPK_MAX, (n_pages * PAGE_SIZE + t_new) // 4)

    def per_b(shape):
        return pl.BlockSpec((None,) + shape, lambda b, j, pt: (b,) + (0,) * len(shape))

    def page(p):
        return pl.BlockSpec((None, None, PAGE_SIZE, IDX_DIM),
                            lambda b, j, pt: (pt[b, j * pps + p], layer, 0, 0))

    grid_spec = pltpu.PrefetchScalarGridSpec(
        num_scalar_prefetch=1,
        grid=(nb, n_pages // pps),
        in_specs=[per_b((IDX_HEADS * TOK_PAD, IDX_DIM)), per_b((TOK_PAD, LANES)), per_b((PAGE_SIZE, IDX_DIM))]
                 + [page(p) for p in range(pps)],
        out_specs=per_b((n_pages + 1, TOK_PAD, LANES)),
        scratch_shapes=[pltpu.VMEM((n_pages + 1, TOK_PAD, LANES), jnp.int32)],
    )
    return pl.pallas_call(
        functools.partial(_sselect_kernel, pps=pps, n_pages=n_pages, t_new=t_new, k_top=k_top),
        grid_spec=grid_spec,
        out_shape=jax.ShapeDtypeStruct((nb, n_pages + 1, TOK_PAD, LANES), F32),
        compiler_params=_params(2),
        name="sample_select",
    )(page_table, qs, misc_bm, kidx_new, *([cache_kidx] * pps))


def _sattn_kernel(pt_ref, qbd_ref, bias_ref, knew_ref, vnew_ref, *rest, pps, n_pages):
    k_refs = rest[:pps]
    v_refs = rest[pps:2 * pps]
    o_ref = rest[2 * pps]
    m_scr, l_scr, acc_scr = rest[2 * pps + 1:]
    j = pl.program_id(1)
    nsteps = n_pages // pps
    rows = ATT_HEADS * TOK_PAD

    @pl.when(j == 0)
    def _():
        m_scr[...] = jnp.full((rows, 1), NEG_BIG, F32)
        l_scr[...] = jnp.zeros((rows, 1), F32)
        acc_scr[...] = jnp.zeros((rows, ATT_WIDTH), F32)

    qbd = qbd_ref[...]

    def update(kp, vp, bias):
        s = _dot_nt(qbd, kp.astype(BF16)) * HEAD_DIM ** -0.5 + jnp.concatenate([bias] * ATT_HEADS, axis=0)
        m = m_scr[...]
        m_new = jnp.maximum(m, jnp.max(s, axis=1, keepdims=True))
        alpha = jnp.exp(m - m_new)
        p = jnp.exp(s - m_new)
        l_scr[...] = l_scr[...] * alpha + jnp.sum(p, axis=1, keepdims=True)
        acc_scr[...] = acc_scr[...] * alpha + _dot(p.astype(BF16), vp.astype(BF16))
        m_scr[...] = m_new

    for p in range(pps):
        update(k_refs[p][...], v_refs[p][...], bias_ref[j * pps + p])

    @pl.when(j == nsteps - 1)
    def _():
        update(knew_ref[...], vnew_ref[...], bias_ref[n_pages])
        full = acc_scr[...] / l_scr[...]
        lane_head = lax.broadcasted_iota(jnp.int32, (TOK_PAD, ATT_WIDTH), 1) // HEAD_DIM
        out = jnp.zeros((TOK_PAD, ATT_WIDTH), F32)
        for h in range(ATT_HEADS):
            out = out + jnp.where(lane_head == h, full[h * TOK_PAD:(h + 1) * TOK_PAD], 0.0)
        o_ref[...] = out


def _sample_attend(page_table, qbd, bias, k_new, v_new, cache_k, cache_v, layer, *, pps):
    nb, n_pages = page_table.shape
    rows = ATT_HEADS * TOK_PAD

    def per_b(shape):
        return pl.BlockSpec((None,) + shape, lambda b, j, pt: (b,) + (0,) * len(shape))

    def page(p):
        return pl.BlockSpec((None, None, PAGE_SIZE, ATT_WIDTH),
                            lambda b, j, pt: (pt[b, j * pps + p], layer, 0, 0))

    grid_spec = pltpu.PrefetchScalarGridSpec(
        num_scalar_prefetch=1,
        grid=(nb, n_pages // pps),
        in_specs=[per_b((rows, ATT_WIDTH)), per_b((n_pages + 1, TOK_PAD, LANES)),
                  per_b((PAGE_SIZE, ATT_WIDTH)), per_b((PAGE_SIZE, ATT_WIDTH))]
                 + [page(p) for p in range(pps)] * 2,
        out_specs=per_b((TOK_PAD, ATT_WIDTH)),
        scratch_shapes=[pltpu.VMEM((rows, 1), F32), pltpu.VMEM((rows, 1), F32), pltpu.VMEM((rows, ATT_WIDTH), F32)],
    )
    return pl.pallas_call(
        functools.partial(_sattn_kernel, pps=pps, n_pages=n_pages),
        grid_spec=grid_spec,
        out_shape=jax.ShapeDtypeStruct((nb, TOK_PAD, ATT_WIDTH), F32),
        compiler_params=_params(2),
        name="sample_attend",
    )(page_table, qbd, bias, k_new, v_new, *([cache_k] * pps), *([cache_v] * pps))


def _post_kernel(x_ref, ys_ref, ya_ref, yc_ref, wo_ref, g1_ref, b1_ref, wu_ref, fcw_ref, fcb_ref, wd_ref,
                 g2_ref, b2_ref, pffn_ref, o_ref, nffn_ref, ubuf, acc_ref, *, tt, rs, nt, cw):
    i = pl.program_id(1)
    h_rows = (FFN_CONV_W - 1) * rs
    hf = _round_up(h_rows, SUBLANES)

    @pl.when(i == 0)
    def _():
        ubuf[hf - h_rows:hf, :] = pffn_ref[...]

    mix = _dot(ys_ref[...].astype(BF16), wo_ref[0:SSD_INNER, :])
    mix = mix + _dot(ya_ref[...].astype(BF16), wo_ref[SSD_INNER:SSD_INNER + ATT_WIDTH, :])
    mix = mix + _dot(yc_ref[...].astype(BF16), wo_ref[SSD_INNER + ATT_WIDTH:, :])
    x1 = _layer_norm(ALPHA * x_ref[...] + mix, g1_ref[...], b1_ref[...])
    x1b = x1.astype(BF16)

    def conv(c0):
        acc = fcb_ref[:, c0:c0 + cw] + fcw_ref[0:1, c0:c0 + cw] * ubuf[hf - h_rows:hf - h_rows + tt, c0:c0 + cw]
        for jj in range(1, FFN_CONV_W):
            off = hf - (FFN_CONV_W - 1 - jj) * rs
            acc = acc + fcw_ref[jj:jj + 1, c0:c0 + cw] * ubuf[off:off + tt, c0:c0 + cw]
        return acc

    for c in range(D_FF // cw):
        cv = c * cw
        cg = D_FF + c * cw
        ubuf[hf:hf + tt, cv:cv + cw] = _dot(x1b, wu_ref[:, cv:cv + cw])
        ubuf[hf:hf + tt, cg:cg + cw] = _dot(x1b, wu_ref[:, cg:cg + cw])
        f = (_silu(conv(cg)) * conv(cv)).astype(BF16)
        part = _dot(f, wd_ref[cv:cv + cw, :])
        if c == 0:
            acc_ref[...] = part
        else:
            acc_ref[...] += part

    o_ref[...] = _layer_norm(ALPHA * x1 + acc_ref[...], g2_ref[...], b2_ref[...])

    @pl.when(i == nt - 1)
    def _():
        nffn_ref[...] = ubuf[hf + tt - h_rows:hf + tt, :]

    ubuf[hf - h_rows:hf, :] = ubuf[hf + tt - h_rows:hf + tt, :]


def _post(x, ys, ya, yc, wl, prev_ffn, *, tt, rs):
    nb, t, _ = x.shape
    nt = t // tt
    hf = _round_up((FFN_CONV_W - 1) * rs, SUBLANES)

    def tile(c):
        return pl.BlockSpec((None, tt, c), lambda b, i: (b, i, 0))

    def const(shape):
        return pl.BlockSpec(shape, lambda b, i: (0,) * len(shape), pipeline_mode=pl.Buffered(1))

    sspec = pl.BlockSpec((None, prev_ffn.shape[1], 2 * D_FF), lambda b, i: (b, 0, 0))
    return pl.pallas_call(
        functools.partial(_post_kernel, tt=tt, rs=rs, nt=nt, cw=256),
        grid=(nb, nt),
        in_specs=[tile(D_MODEL), tile(SSD_INNER), tile(ATT_WIDTH), tile(CC_WIDTH),
                  const((D_MODEL, D_MODEL)), const((1, D_MODEL)), const((1, D_MODEL)),
                  const((D_MODEL, 2 * D_FF)), const((FFN_CONV_W, 2 * D_FF)), const((1, 2 * D_FF)),
                  const((D_FF, D_MODEL)), const((1, D_MODEL)), const((1, D_MODEL)), sspec],
        out_specs=[tile(D_MODEL), sspec],
        out_shape=[jax.ShapeDtypeStruct((nb, t, D_MODEL), F32), jax.ShapeDtypeStruct(prev_ffn.shape, F32)],
        scratch_shapes=[pltpu.VMEM((hf + tt, 2 * D_FF), F32), pltpu.VMEM((tt, D_MODEL), F32)],
        compiler_params=_params(2),
        name="post",
    )(x, ys, ya, yc, wl["w_out"], wl["ln1_g"], wl["ln1_b"], wl["ffn_w_up"], wl["ffn_conv_w"], wl["ffn_conv_b"],
      wl["ffn_w_down"], wl["ln2_g"], wl["ln2_b"], prev_ffn)


def _pack_w_in(w_in):
    z, xbc, dt, q, k, v, qi, ki, wi, glu = jnp.split(w_in, [int(c) for c in np.cumsum(IN_SIZES)[:-1]], axis=-1)
    d = w_in.shape[0]
    misc = jnp.concatenate([ki, wi, jnp.zeros((d, D_MODEL, LANES - IDX_DIM - IDX_HEADS), w_in.dtype)], axis=-1)
    dtp = jnp.concatenate([dt, jnp.zeros((d, D_MODEL, LANES - SSD_HEADS), w_in.dtype)], axis=-1)
    return jnp.concatenate([z, xbc, q, k, v, qi, glu, misc, dtp], axis=-1).astype(BF16)


def _layer_weights(p, l):
    pad_l = LANES - SSD_HEADS
    return {
        "w_in": p["w_in_packed"][l],
        "ssd_conv_w": p["ssd_conv_w"][l], "ssd_conv_b": p["ssd_conv_b"][l][None],
        "cc_conv_w": p["cc_conv_w"][l], "cc_conv_b": p["cc_conv_b"][l][None],
        "cc_ln_g": p["cc_ln_g"][l][None], "cc_ln_b": p["cc_ln_b"][l][None],
        "dt_bias": jnp.pad(p["ssd_dt_bias"][l], (0, pad_l))[None],
        "a_log": jnp.pad(p["ssd_a_log"][l], (0, pad_l))[None],
        "ssd_d": jnp.repeat(p["ssd_d"][l], HEAD_DIM)[None],
        "ssd_norm_g": p["ssd_norm_g"][l][None],
        "w_out": p["w_out_bf"][l], "ln1_g": p["ln1_g"][l][None], "ln1_b": p["ln1_b"][l][None],
        "ffn_w_up": p["ffn_w_up_bf"][l], "ffn_conv_w": p["ffn_conv_w"][l], "ffn_conv_b": p["ffn_conv_b"][l][None],
        "ffn_w_down": p["ffn_w_down_bf"][l], "ln2_g": p["ln2_g"][l][None], "ln2_b": p["ln2_b"][l][None],
    }


def _rope_tables(pos):
    half = HEAD_DIM // 2
    inv = ROPE_THETA ** (-jnp.arange(half, dtype=F32) / half)
    ang = pos.astype(F32)[:, None] * inv[None, :]
    cos = jnp.cos(ang)
    sin = jnp.sin(ang)
    cos_h = jnp.concatenate([cos, cos], axis=-1)
    sin_h = jnp.concatenate([-sin, sin], axis=-1)
    n = pos.shape[0]
    pad1 = jnp.ones((n, LANES - IDX_DIM), F32)
    pad0 = jnp.zeros((n, LANES - IDX_DIM), F32)
    return {"cosq": jnp.tile(cos_h, (1, ATT_HEADS)), "sinq": jnp.tile(sin_h, (1, ATT_HEADS)),
            "cosm": jnp.concatenate([cos_h, pad1], axis=-1), "sinm": jnp.concatenate([sin_h, pad0], axis=-1)}


def _prompt_layer(h, wl, tabs, *, tt_pre, tt_post, tq, ck):
    nb, t, _ = h.shape
    zeros = lambda rows, c: jnp.zeros((nb, rows, c), F32)
    (z, xbc, q, k, v, qi, misc, kidx, dt, ycc, nssd, ncc) = _premix(
        h, wl, tabs, zeros(SSD_CONV_W - 1, SSD_CONV_DIM), zeros(CC_CONV_W - 1, CC_WIDTH), tt=tt_pre, rs=1)
    yssd, ssm = _ssd(xbc, dt, z, jnp.zeros((nb, SSD_HEADS, HEAD_DIM, SSD_STATE), F32), wl)
    yatt = _prompt_attend(qi, misc, q, kidx, k, v, tq=tq, ck=ck)
    h, nffn = _post(h, yssd, yatt, ycc, wl, zeros(FFN_CONV_W - 1, 2 * D_FF), tt=tt_post, rs=1)
    states = (k.reshape(nb, t, ATT_HEADS, HEAD_DIM), v.reshape(nb, t, ATT_HEADS, HEAD_DIM), kidx, ssm,
              nssd, ncc, nffn)
    return h, states


def _to_time_major(state):
    b, r, c = state.shape
    return state.transpose(1, 0, 2).reshape(1, r * b, c)


def _to_batch_major(rows, nb):
    _, n, c = rows.shape
    return rows.reshape(n // nb, nb, c).transpose(1, 0, 2)


def _pad_rows(a, rows):
    return jnp.pad(a, ((0, 0), (0, rows - a.shape[1]), (0, 0)))


def _sample_layer(g, wl, tabs, l, cache_k, cache_v, cache_kidx, page_table, ssm_prev, ssd_prev, cc_prev,
                  ffn_prev, *, nb, t_new, pps):
    (z, xbc, q, k, v, qi, misc, kidx, dt, ycc, nssd, ncc) = _premix(
        g, wl, tabs, _to_time_major(ssd_prev), _to_time_major(cc_prev), tt=nb * t_new, rs=nb)
    bm = lambda a: _to_batch_major(a, nb)
    yssd, ssm = _ssd(_pad_rows(bm(xbc), SSD_CHUNK), _pad_rows(bm(dt), SSD_CHUNK), _pad_rows(bm(z), SSD_CHUNK),
                     ssm_prev, wl, t_valid=t_new)
    yssd = _to_time_major(yssd[:, :t_new])
    qi_bm = _pad_rows(bm(qi), TOK_PAD).reshape(nb, TOK_PAD, IDX_HEADS, IDX_DIM)
    qs = qi_bm.transpose(0, 2, 1, 3).reshape(nb, IDX_HEADS * TOK_PAD, IDX_DIM)
    bias = _sample_select(page_table, qs, _pad_rows(bm(misc), TOK_PAD), _pad_rows(bm(kidx), PAGE_SIZE),
                          cache_kidx, l, pps=pps, t_new=t_new)
    q_bm = _pad_rows(bm(q), TOK_PAD).reshape(nb, TOK_PAD, ATT_HEADS, HEAD_DIM)
    eye = jnp.eye(ATT_HEADS, dtype=q_bm.dtype)
    qbd = jnp.einsum("bthd,hg->bhtgd", q_bm, eye).reshape(nb, ATT_HEADS * TOK_PAD, ATT_WIDTH)
    k_bm = bm(k)
    v_bm = bm(v)
    paged = cache_k.shape[:3] + (ATT_WIDTH,)
    yatt = _sample_attend(page_table, qbd, bias, _pad_rows(k_bm, PAGE_SIZE), _pad_rows(v_bm, PAGE_SIZE),
                          cache_k.reshape(paged), cache_v.reshape(paged), l, pps=pps)
    yatt = _to_time_major(yatt[:, :t_new])
    g, nffn = _post(g, yssd, yatt, ycc, wl, _to_time_major(ffn_prev), tt=nb * t_new, rs=nb)
    states = (k_bm.reshape(nb, t_new, ATT_HEADS, HEAD_DIM), v_bm.reshape(nb, t_new, ATT_HEADS, HEAD_DIM),
              bm(kidx), ssm, bm(nssd), bm(ncc), bm(nffn))
    return g, states


def kernel(x_prompt, x_sample, cache_k, cache_v, cache_kidx, page_table, state_ssm, state_ssd_conv, state_cc_conv, state_ffn_conv, ln0_g, ln0_b, w_in, ssd_conv_w, ssd_conv_b, ssd_dt_bias, ssd_a_log, ssd_d, ssd_norm_g, cc_conv_w, cc_conv_b, cc_ln_g, cc_ln_b, w_out, ln1_g, ln1_b, ffn_w_up, ffn_conv_w, ffn_conv_b, ffn_w_down, ln2_g, ln2_b):
    depth = w_in.shape[0]
    p = {"w_in_packed": _pack_w_in(w_in), "ssd_conv_w": ssd_conv_w, "ssd_conv_b": ssd_conv_b,
         "ssd_dt_bias": ssd_dt_bias, "ssd_a_log": ssd_a_log, "ssd_d": ssd_d, "ssd_norm_g": ssd_norm_g,
         "cc_conv_w": cc_conv_w, "cc_conv_b": cc_conv_b, "cc_ln_g": cc_ln_g, "cc_ln_b": cc_ln_b,
         "w_out_bf": w_out.astype(BF16), "ln1_g": ln1_g, "ln1_b": ln1_b,
         "ffn_w_up_bf": ffn_w_up.astype(BF16), "ffn_conv_w": ffn_conv_w, "ffn_conv_b": ffn_conv_b,
         "ffn_w_down_bf": ffn_w_down.astype(BF16), "ln2_g": ln2_g, "ln2_b": ln2_b}
    layers = [_layer_weights(p, l) for l in range(depth)]

    bp, sp, _ = x_prompt.shape
    tt_pre = min(256, sp)
    h = _ln_rows(x_prompt.reshape(bp * sp, D_MODEL), ln0_g, ln0_b, tt_pre).reshape(bp, sp, D_MODEL)
    tabs_p = _rope_tables(jnp.arange(sp))
    p_states = []
    for l in range(depth):
        h, st = _prompt_layer(h, layers[l], tabs_p, tt_pre=tt_pre, tt_post=min(256, sp), tq=128,
                              ck=min(512, sp))
        p_states.append(st)

    nb, ts, _ = x_sample.shape
    n_pages = page_table.shape[1]
    pps = math.gcd(8, n_pages)
    g = _ln_rows(x_sample.transpose(1, 0, 2).reshape(ts * nb, D_MODEL), ln0_g, ln0_b, ts * nb)
    g = g.reshape(1, ts * nb, D_MODEL)
    tabs_s = _rope_tables(PAST_LEN + jnp.arange(ts * nb) // nb)
    s_states = []
    for l in range(depth):
        g, st = _sample_layer(g, layers[l], tabs_s, l, cache_k, cache_v, cache_kidx, page_table,
                              state_ssm[:, l], state_ssd_conv[:, l], state_cc_conv[:, l], state_ffn_conv[:, l],
                              nb=nb, t_new=ts, pps=pps)
        s_states.append(st)
    y_sample = _to_batch_major(g, nb)

    p_out = [jnp.stack(a, axis=1) for a in zip(*p_states)]
    s_out = [jnp.stack(a, axis=1) for a in zip(*s_states)]
    return (h, y_sample, *p_out, *s_out)
```

```python
import functools
import math

import jax
import jax.numpy as jnp
import numpy as np
from jax import lax
from jax.experimental import pallas as pl
from jax.experimental.pallas import tpu as pltpu

D_MODEL = 1024
DEPTH = 4
PAST_LEN = 8192
PAGE_SIZE = 128
HEAD_DIM = 64
SSD_HEADS = 8
SSD_INNER = SSD_HEADS * HEAD_DIM
SSD_GROUPS = 2
SSD_STATE = 64
SSD_CONV_W = 4
SSD_CONV_DIM = SSD_INNER + 2 * SSD_GROUPS * SSD_STATE
SSD_CHUNK = 128
ATT_HEADS = 4
ATT_WIDTH = ATT_HEADS * HEAD_DIM
IDX_HEADS = 4
IDX_DIM = 64
TOPK_MAX = 256
ROPE_THETA = 10000.0
CC_WIDTH = 256
CC_CONV_W = 31
D_FF = 2816
FFN_CONV_W = 3
ALPHA = (2 * DEPTH) ** 0.25
EPS = 1e-5
IN_SIZES = (SSD_INNER, SSD_CONV_DIM, SSD_HEADS, ATT_WIDTH, ATT_WIDTH, ATT_WIDTH,
            IDX_HEADS * IDX_DIM, IDX_DIM, IDX_HEADS, 2 * CC_WIDTH)

LANES = 128
SUBLANES = 8
VMEM_LIMIT = 56 * 1024 * 1024

C_Z = 0
C_XBC = C_Z + SSD_INNER
C_Q = C_XBC + SSD_CONV_DIM
C_K = C_Q + ATT_WIDTH
C_V = C_K + ATT_WIDTH
C_QI = C_V + ATT_WIDTH
C_GLU = C_QI + IDX_HEADS * IDX_DIM
C_MISC = C_GLU + 2 * CC_WIDTH
C_DT = C_MISC + LANES
C_END = C_DT + LANES
WI_LANE = IDX_DIM

NEG_BIG = -1e30
KEY_NEG_INF = np.int32(-2139095041)
INT32_MIN = np.int32(-2 ** 31)

F32 = jnp.float32
BF16 = jnp.bfloat16
NT_DIMS = (((1,), (1,)), ((), ()))


def _round_up(n, m):
    return (n + m - 1) // m * m


def _silu(x):
    return x * jax.nn.sigmoid(x)


def _layer_norm(x, g, b):
    mu = jnp.mean(x, -1, keepdims=True)
    var = jnp.mean(jnp.square(x - mu), -1, keepdims=True)
    return (x - mu) * lax.rsqrt(var + EPS) * g + b


def _dot(a, b):
    return jnp.dot(a, b, preferred_element_type=F32)


def _dot_nt(a, b):
    return lax.dot_general(a, b, NT_DIMS, preferred_element_type=F32)


def _params(n_axes):
    return pltpu.CompilerParams(dimension_semantics=("arbitrary",) * n_axes, vmem_limit_bytes=VMEM_LIMIT)


def _rope(x, cos, sin_signed):
    n = x.shape[-1]
    lane = lax.broadcasted_iota(jnp.int32, x.shape, 1)
    first_half = (lane % HEAD_DIM) < (HEAD_DIM // 2)
    partner = jnp.where(first_half, pltpu.roll(x, n - HEAD_DIM // 2, 1), pltpu.roll(x, HEAD_DIM // 2, 1))
    return x * cos + partner * sin_signed


def _score_key(score):
    score = jnp.where(score == 0.0, 0.0, score)
    bits = pltpu.bitcast(score, jnp.int32)
    return jnp.where(bits < 0, bits ^ np.int32(0x7FFFFFFF), bits)


def _split3(x):
    hi = x.astype(BF16)
    r1 = x - hi.astype(F32)
    mid = r1.astype(BF16)
    lo = (r1 - mid.astype(F32)).astype(BF16)
    return hi, mid, lo


def _ln_kernel(x_ref, g_ref, b_ref, o_ref):
    o_ref[...] = _layer_norm(x_ref[...], g_ref[...], b_ref[...])


def _ln_rows(x2d, g, b, rows):
    n, d = x2d.shape
    return pl.pallas_call(
        _ln_kernel,
        grid=(n // rows,),
        in_specs=[pl.BlockSpec((rows, d), lambda i: (i, 0)),
                  pl.BlockSpec((1, d), lambda i: (0, 0)),
                  pl.BlockSpec((1, d), lambda i: (0, 0))],
        out_specs=pl.BlockSpec((rows, d), lambda i: (i, 0)),
        out_shape=jax.ShapeDtypeStruct((n, d), F32),
        compiler_params=_params(1),
        name="ln0",
    )(x2d, g.reshape(1, d), b.reshape(1, d))


def _premix_kernel(x_ref, w_ref, scw_ref, scb_ref, ccw_ref, ccb_ref, lng_ref, lnb_ref, dtb_ref,
                   cosq_ref, sinq_ref, cosm_ref, sinm_ref, pssd_ref, pcc_ref,
                   z_ref, xbc_ref, q_ref, k_ref, v_ref, qi_ref, misc_ref, kidx_ref, dt_ref, ycc_ref,
                   qt_ref, qit_ref, misct_ref, kbf_ref, kidxbf_ref, vt_ref,
                   nssd_ref, ncc_ref, sbuf, cbuf, *, tt, rs, nt, rb):
    i = pl.program_id(1)
    hs_rows = (SSD_CONV_W - 1) * rs
    hc_rows = (CC_CONV_W - 1) * rs
    hs = _round_up(hs_rows, SUBLANES)
    hc = _round_up(hc_rows, SUBLANES)

    @pl.when(i == 0)
    def _():
        sbuf[hs - hs_rows:hs, :] = pssd_ref[...]
        cbuf[hc - hc_rows:hc, :] = pcc_ref[...]

    xb = x_ref[...].astype(BF16)

    def seg(a, b):
        return _dot(xb, w_ref[:, a:b])

    z_ref[...] = seg(C_Z, C_XBC)

    sbuf[hs:hs + tt, :] = seg(C_XBC, C_Q)
    for r0 in range(0, tt, rb):
        acc = scb_ref[...] + scw_ref[0:1, :] * sbuf[hs - hs_rows + r0:hs - hs_rows + r0 + rb, :]
        for j in range(1, SSD_CONV_W):
            off = hs - (SSD_CONV_W - 1 - j) * rs + r0
            acc = acc + scw_ref[j:j + 1, :] * sbuf[off:off + rb, :]
        xbc_ref[r0:r0 + rb, :] = _silu(acc)

    @pl.when(i == nt - 1)
    def _():
        nssd_ref[...] = sbuf[hs + tt - hs_rows:hs + tt, :]

    sbuf[hs - hs_rows:hs, :] = sbuf[hs + tt - hs_rows:hs + tt, :]

    dtr = seg(C_DT, C_END) + dtb_ref[...]
    sp = jnp.maximum(dtr, 0.0) + jnp.log1p(jnp.exp(-jnp.abs(dtr)))
    lane = lax.broadcasted_iota(jnp.int32, sp.shape, 1)
    dt_ref[...] = jnp.where(lane < SSD_HEADS, sp, 0.0)

    cosq = cosq_ref[...]
    sinq = sinq_ref[...]
    q = _rope(seg(C_Q, C_K), cosq, sinq)
    q_ref[...] = q.astype(BF16)
    qt_ref[...] = q.T.astype(BF16)
    k = _rope(seg(C_K, C_V), cosq, sinq)
    k_ref[...] = k
    kbf_ref[...] = k.astype(BF16)
    v = seg(C_V, C_QI)
    v_ref[...] = v
    vt_ref[...] = v.T.astype(BF16)
    qi = _rope(seg(C_QI, C_GLU), cosq, sinq)
    qi_ref[...] = qi.astype(BF16)
    qit_ref[...] = qi.T.astype(BF16)
    misc = _rope(seg(C_MISC, C_DT), cosm_ref[...], sinm_ref[...])
    misc_ref[...] = misc
    misct_ref[...] = misc.T
    kidx_ref[...] = misc[:, :IDX_DIM]
    kidxbf_ref[...] = misc[:, :IDX_DIM].astype(BF16)

    glu = seg(C_GLU, C_MISC)
    cbuf[hc:hc + tt, :] = glu[:, :CC_WIDTH] * jax.nn.sigmoid(glu[:, CC_WIDTH:])
    for r0 in range(0, tt, rb):
        acc = ccb_ref[...] + ccw_ref[0:1, :] * cbuf[hc - hc_rows + r0:hc - hc_rows + r0 + rb, :]
        for j in range(1, CC_CONV_W):
            off = hc - (CC_CONV_W - 1 - j) * rs + r0
            acc = acc + ccw_ref[j:j + 1, :] * cbuf[off:off + rb, :]
        ycc_ref[r0:r0 + rb, :] = _silu(_layer_norm(acc, lng_ref[...], lnb_ref[...]))

    @pl.when(i == nt - 1)
    def _():
        ncc_ref[...] = cbuf[hc + tt - hc_rows:hc + tt, :]

    cbuf[hc - hc_rows:hc, :] = cbuf[hc + tt - hc_rows:hc + tt, :]


def _premix(x, wl, tabs, prev_ssd, prev_cc, *, tt, rs):
    nb, t, _ = x.shape
    nt = t // tt
    rb = min(tt, 64)
    hs = _round_up((SSD_CONV_W - 1) * rs, SUBLANES)
    hc = _round_up((CC_CONV_W - 1) * rs, SUBLANES)

    def tile(c):
        return pl.BlockSpec((None, tt, c), lambda b, i: (b, i, 0))

    def const(shape):
        return pl.BlockSpec(shape, lambda b, i: (0,) * len(shape))

    def tab(c):
        return pl.BlockSpec((tt, c), lambda b, i: (i, 0))

    def state(rows, c):
        return pl.BlockSpec((None, rows, c), lambda b, i: (b, 0, 0))

    widths = (SSD_INNER, SSD_CONV_DIM, ATT_WIDTH, ATT_WIDTH, ATT_WIDTH, IDX_HEADS * IDX_DIM,
              LANES, IDX_DIM, LANES, CC_WIDTH)
    dtypes = (F32, F32, BF16, F32, F32, BF16, F32, F32, F32, F32)
    out_shape = [jax.ShapeDtypeStruct((nb, t, c), d) for c, d in zip(widths, dtypes)]
    out_specs = [tile(c) for c in widths]

    def tile_t(c):
        return pl.BlockSpec((None, c, tt), lambda b, i: (b, 0, i))

    out_shape += [jax.ShapeDtypeStruct((nb, ATT_WIDTH, t), BF16),
                  jax.ShapeDtypeStruct((nb, IDX_HEADS * IDX_DIM, t), BF16),
                  jax.ShapeDtypeStruct((nb, LANES, t), F32),
                  jax.ShapeDtypeStruct((nb, t, ATT_WIDTH), BF16),
                  jax.ShapeDtypeStruct((nb, t, IDX_DIM), BF16),
                  jax.ShapeDtypeStruct((nb, nt, ATT_WIDTH, tt), BF16)]
    out_specs += [tile_t(ATT_WIDTH), tile_t(IDX_HEADS * IDX_DIM), tile_t(LANES), tile(ATT_WIDTH), tile(IDX_DIM),
                  pl.BlockSpec((None, None, ATT_WIDTH, tt), lambda b, i: (b, i, 0, 0))]
    out_shape += [jax.ShapeDtypeStruct(prev_ssd.shape, F32), jax.ShapeDtypeStruct(prev_cc.shape, F32)]
    out_specs += [state(prev_ssd.shape[1], SSD_CONV_DIM), state(prev_cc.shape[1], CC_WIDTH)]
    return pl.pallas_call(
        functools.partial(_premix_kernel, tt=tt, rs=rs, nt=nt, rb=rb),
        grid=(nb, nt),
        in_specs=[tile(D_MODEL), const((D_MODEL, C_END)),
                  const((SSD_CONV_W, SSD_CONV_DIM)), const((1, SSD_CONV_DIM)),
                  const((CC_CONV_W, CC_WIDTH)), const((1, CC_WIDTH)), const((1, CC_WIDTH)), const((1, CC_WIDTH)),
                  const((1, LANES)),
                  tab(ATT_WIDTH), tab(ATT_WIDTH), tab(LANES), tab(LANES),
                  state(prev_ssd.shape[1], SSD_CONV_DIM), state(prev_cc.shape[1], CC_WIDTH)],
        out_specs=out_specs,
        out_shape=out_shape,
        scratch_shapes=[pltpu.VMEM((hs + tt, SSD_CONV_DIM), F32), pltpu.VMEM((hc + tt, CC_WIDTH), F32)],
        compiler_params=_params(2),
        name="premix",
    )(x, wl["w_in"], wl["ssd_conv_w"], wl["ssd_conv_b"], wl["cc_conv_w"], wl["cc_conv_b"],
      wl["cc_ln_g"], wl["cc_ln_b"], wl["dt_bias"], tabs["cosq"], tabs["sinq"], tabs["cosm"], tabs["sinm"],
      prev_ssd, prev_cc)


def _ssd_kernel(xbc_ref, dt_ref, z_ref, h0_ref, alog_ref, drow_ref, g_ref, y_ref, hout_ref, h_scr, y_scr,
                *, nt, t_valid):
    i = pl.program_id(1)
    q = SSD_CHUNK

    @pl.when(i == 0)
    def _():
        h_scr[...] = h0_ref[...]

    row = lax.broadcasted_iota(jnp.int32, (q, q), 0)
    col = lax.broadcasted_iota(jnp.int32, (q, q), 1)
    causal = col <= row
    tril = jnp.where(causal, 1.0, 0.0).astype(BF16)

    dt = dt_ref[...]
    if t_valid is not None:
        dt = jnp.where(row < t_valid, dt, 0.0)
    lane1 = lax.broadcasted_iota(jnp.int32, (1, LANES), 1)
    a_row = jnp.where(lane1 < SSD_HEADS, -jnp.exp(alog_ref[...]), 0.0)
    da = dt * a_row
    hi, mid, lo = _split3(da)
    cum = _dot(tril, hi) + _dot(tril, mid) + _dot(tril, lo)
    cum_t = cum.T
    last = cum[q - 1:q, :]
    w_s = jnp.exp(last - cum)
    e_cum = jnp.exp(cum)
    e_last = jnp.exp(last)

    erow = lax.broadcasted_iota(jnp.int32, (LANES, SSD_INNER), 0)
    ecol = lax.broadcasted_iota(jnp.int32, (LANES, SSD_INNER), 1)
    expand = jnp.where(ecol // HEAD_DIM == erow, 1.0, 0.0).astype(BF16)
    dhi, dmid, dlo = _split3(dt)
    dt_wide = _dot(dhi, expand) + _dot(dmid, expand) + _dot(dlo, expand)

    xs = xbc_ref[:, :SSD_INNER]
    xdt = xs * dt_wide
    xdt_t = xdt.T
    gn = SSD_GROUPS * SSD_STATE
    rep = SSD_HEADS // SSD_GROUPS
    for g in range(SSD_GROUPS):
        b_g = xbc_ref[:, SSD_INNER + g * SSD_STATE:SSD_INNER + (g + 1) * SSD_STATE]
        c_g = xbc_ref[:, SSD_INNER + gn + g * SSD_STATE:SSD_INNER + gn + (g + 1) * SSD_STATE]
        c_bf = c_g.astype(BF16)
        cb = _dot_nt(c_bf, b_g.astype(BF16))
        for h in range(g * rep, (g + 1) * rep):
            sl = slice(h * HEAD_DIM, (h + 1) * HEAD_DIM)
            seg = cum[:, h:h + 1] - cum_t[h:h + 1, :]
            decay = jnp.exp(jnp.where(causal, seg, -jnp.inf))
            m = (cb * decay).astype(BF16)
            h_prev = h_scr[h]
            y = _dot(m, xdt[:, sl].astype(BF16))
            y = y + _dot_nt(c_bf, h_prev.astype(BF16)) * e_cum[:, h:h + 1]
            bw = (b_g * w_s[:, h:h + 1]).astype(BF16)
            h_scr[h] = h_prev * e_last[:, h:h + 1] + _dot(xdt_t[sl, :].astype(BF16), bw)
            y_scr[:, sl] = y

    y = (y_scr[...] + drow_ref[...] * xs) * _silu(z_ref[...])
    y_ref[...] = y * lax.rsqrt(jnp.mean(y * y, -1, keepdims=True) + EPS) * g_ref[...]

    @pl.when(i == nt - 1)
    def _():
        hout_ref[...] = h_scr[...]


def _ssd(xbc, dt, z, h0, wl, *, t_valid=None):
    nb, t, _ = xbc.shape
    nt = t // SSD_CHUNK

    def tile(c):
        return pl.BlockSpec((None, SSD_CHUNK, c), lambda b, i: (b, i, 0))

    def const(c):
        return pl.BlockSpec((1, c), lambda b, i: (0, 0))

    hspec = pl.BlockSpec((None, SSD_HEADS, HEAD_DIM, SSD_STATE), lambda b, i: (b, 0, 0, 0))
    return pl.pallas_call(
        functools.partial(_ssd_kernel, nt=nt, t_valid=t_valid),
        grid=(nb, nt),
        in_specs=[tile(SSD_CONV_DIM), tile(LANES), tile(SSD_INNER), hspec,
                  const(LANES), const(SSD_INNER), const(SSD_INNER)],
        out_specs=[tile(SSD_INNER), hspec],
        out_shape=[jax.ShapeDtypeStruct((nb, t, SSD_INNER), F32),
                   jax.ShapeDtypeStruct((nb, SSD_HEADS, HEAD_DIM, SSD_STATE), F32)],
        scratch_shapes=[pltpu.VMEM((SSD_HEADS, HEAD_DIM, SSD_STATE), F32),
                        pltpu.VMEM((SSD_CHUNK, SSD_INNER), F32)],
        compiler_params=_params(2),
        name="ssd",
    )(xbc, dt, z, h0, wl["a_log"], wl["ssd_d"], wl["ssd_norm_g"])


def _kth_largest(count_ge, shape, k_top):
    def body(it, prefix):
        cand = prefix ^ lax.shift_left(jnp.int32(1), 31 - it)
        return jnp.where(count_ge(cand) >= k_top, cand, prefix)
    return lax.fori_loop(0, 32, body, jnp.full(shape, INT32_MIN, jnp.int32))


def _tri(lower):
    r = lax.broadcasted_iota(jnp.int32, (LANES, LANES), 0)
    c = lax.broadcasted_iota(jnp.int32, (LANES, LANES), 1)
    return jnp.where((c <= r) if lower else (r <= c), 1.0, 0.0).astype(BF16)


def _pattn_kernel(qit_ref, qt_ref, misct_ref, kidx_ref, k_ref, vt_ref, o_ref, keys_scr, *, tq, ck, k_top):
    i = pl.program_id(1)
    nch = (i * tq + tq + ck - 1) // ck
    sub = ck // LANES

    qit = qit_ref[...]
    qi_all = jnp.concatenate([qit[h * IDX_DIM:(h + 1) * IDX_DIM, :] for h in range(IDX_HEADS)], axis=1)
    w_rows = misct_ref[WI_LANE:WI_LANE + SUBLANES, :] * IDX_HEADS ** -0.5
    q_pos = i * tq + lax.broadcasted_iota(jnp.int32, (ck, tq), 1)
    k_row = lax.broadcasted_iota(jnp.int32, (ck, tq), 0)

    def score_chunk(c, carry):
        base = pl.multiple_of(c * ck, ck)
        st = _dot(kidx_ref[pl.ds(base, ck), :], qi_all)
        sc = jnp.zeros((ck, tq), F32)
        for h in range(IDX_HEADS):
            sc = sc + jnp.maximum(st[:, h * tq:(h + 1) * tq] * IDX_DIM ** -0.5, 0.0) * w_rows[h:h + 1, :]
        keys_scr[c] = jnp.where(k_row + base <= q_pos, _score_key(sc), KEY_NEG_INF)
        return carry

    lax.fori_loop(0, nch, score_chunk, 0)

    def count(pred):
        def body(c, acc):
            for j in range(sub):
                hit = jnp.where(pred(keys_scr[c, j * LANES:(j + 1) * LANES, :]), 1, 0)
                acc = acc + hit.reshape(LANES // SUBLANES, SUBLANES, tq).sum(axis=0)
            return acc
        acc = lax.fori_loop(0, nch, body, jnp.zeros((SUBLANES, tq), jnp.int32))
        return jnp.sum(acc, axis=0, keepdims=True)

    thr = _kth_largest(lambda cand: count(lambda kk: kk >= cand), (1, tq), k_top)
    need = k_top - count(lambda kk: kk > thr)
    n_tie = count(lambda kk: kk == thr)
    surplus = jnp.max(jnp.where((n_tie > need) & (thr > KEY_NEG_INF), 1, 0))

    @pl.when(surplus > 0)
    def _():
        ltri = _tri(lower=True)
        need_f = need.astype(F32)

        def tie_chunk(c, carry):
            for j in range(sub):
                kk = keys_scr[c, j * LANES:(j + 1) * LANES, :]
                eq = kk == thr
                eqf = jnp.where(eq, 1.0, 0.0)
                pc = _dot(ltri, eqf.astype(BF16)) + carry
                keys_scr[c, j * LANES:(j + 1) * LANES, :] = jnp.where(eq & (pc > need_f), KEY_NEG_INF, kk)
                carry = carry + jnp.sum(eqf, axis=0, keepdims=True)
            return carry

        lax.fori_loop(0, nch, tie_chunk, jnp.zeros((1, tq), F32))

    thr_sel = jnp.maximum(thr, KEY_NEG_INF + 1)

    qt = qt_ref[...] * HEAD_DIM ** -0.5
    row_head = lax.broadcasted_iota(jnp.int32, (ATT_WIDTH, tq), 0) // HEAD_DIM
    q_bd = jnp.concatenate([jnp.where(row_head == h, qt, 0.0).astype(BF16) for h in range(ATT_HEADS)], axis=1)

    def att_chunk(c, st):
        ms, ls, accs = st
        base = pl.multiple_of(c * ck, ck)
        s_all = _dot(k_ref[pl.ds(base, ck), :], q_bd)
        bias = jnp.where(keys_scr[c] >= thr_sel, 0.0, NEG_BIG)
        vt = vt_ref[c]
        new_m, new_l, new_acc = [], [], []
        for h in range(ATT_HEADS):
            s = s_all[:, h * tq:(h + 1) * tq] + bias
            m_new = jnp.maximum(ms[h], jnp.max(s, axis=0, keepdims=True))
            alpha = jnp.exp(ms[h] - m_new)
            p = jnp.exp(s - m_new)
            new_m.append(m_new)
            new_l.append(ls[h] * alpha + jnp.sum(p, axis=0, keepdims=True))
            new_acc.append(accs[h] * alpha + _dot(vt[h * HEAD_DIM:(h + 1) * HEAD_DIM, :], p.astype(BF16)))
        return tuple(new_m), tuple(new_l), tuple(new_acc)

    init = (tuple(jnp.full((1, tq), NEG_BIG, F32) for _ in range(ATT_HEADS)),
            tuple(jnp.zeros((1, tq), F32) for _ in range(ATT_HEADS)),
            tuple(jnp.zeros((HEAD_DIM, tq), F32) for _ in range(ATT_HEADS)))
    _, ls, accs = lax.fori_loop(0, nch, att_chunk, init)
    out_t = jnp.concatenate([accs[h] / ls[h] for h in range(ATT_HEADS)], axis=0)
    o_ref[...] = out_t.T


def _prompt_attend(qit, qt, misct, kidx_bf, k_bf, vt, *, tq):
    nb, nchunks, _, ck = vt.shape
    t = nchunks * ck
    k_top = min(TOPK_MAX, t // 4)

    def tile_t(c):
        return pl.BlockSpec((None, c, tq), lambda b, i: (b, 0, i))

    def full(c):
        return pl.BlockSpec((None, t, c), lambda b, i: (b, 0, 0))

    return pl.pallas_call(
        functools.partial(_pattn_kernel, tq=tq, ck=ck, k_top=k_top),
        grid=(nb, t // tq),
        in_specs=[tile_t(IDX_HEADS * IDX_DIM), tile_t(ATT_WIDTH), tile_t(LANES),
                  full(IDX_DIM), full(ATT_WIDTH),
                  pl.BlockSpec((None, nchunks, ATT_WIDTH, ck), lambda b, i: (b, 0, 0, 0))],
        out_specs=pl.BlockSpec((None, tq, ATT_WIDTH), lambda b, i: (b, i, 0)),
        out_shape=jax.ShapeDtypeStruct((nb, t, ATT_WIDTH), F32),
        scratch_shapes=[pltpu.VMEM((nchunks, ck, tq), jnp.int32)],
        compiler_params=_params(2),
        name="prompt_attend",
    )(qit, qt, misct, kidx_bf, k_bf, vt)


TOK_PAD = 8


def _sscore_kernel(pt_ref, qs_ref, misc_ref, knew_ref, *rest, pps, n_pages, t_new):
    page_refs = rest[:pps]
    keys_ref = rest[pps]
    j = pl.program_id(1)
    nsteps = n_pages // pps
    qs = qs_ref[...]
    wcol = misc_ref[:, WI_LANE:WI_LANE + IDX_HEADS] * IDX_HEADS ** -0.5

    def score_keys(kp):
        s = _dot_nt(qs, kp.astype(BF16)) * IDX_DIM ** -0.5
        sc = jnp.zeros((TOK_PAD, kp.shape[0]), F32)
        for h in range(IDX_HEADS):
            sc = sc + jnp.maximum(s[h * TOK_PAD:(h + 1) * TOK_PAD], 0.0) * wcol[:, h:h + 1]
        return _score_key(sc)

    keys = score_keys(jnp.concatenate([r[...] for r in page_refs], axis=0))
    for p in range(pps):
        keys_ref[j * pps + p] = keys[:, p * PAGE_SIZE:(p + 1) * PAGE_SIZE]

    @pl.when(j == nsteps - 1)
    def _():
        row = lax.broadcasted_iota(jnp.int32, (TOK_PAD, LANES), 0)
        col = lax.broadcasted_iota(jnp.int32, (TOK_PAD, LANES), 1)
        keys_ref[n_pages] = jnp.where((col <= row) & (col < t_new), score_keys(knew_ref[...]), KEY_NEG_INF)


def _sample_scores(page_table, qs, misc_bm, kidx_new, cache_kidx, layer, *, pps, t_new):
    nb, n_pages = page_table.shape

    def per_b(shape):
        return pl.BlockSpec((None,) + shape, lambda b, j, pt: (b,) + (0,) * len(shape))

    def page(p):
        return pl.BlockSpec((None, None, PAGE_SIZE, IDX_DIM),
                            lambda b, j, pt: (pt[b, j * pps + p], layer, 0, 0))

    grid_spec = pltpu.PrefetchScalarGridSpec(
        num_scalar_prefetch=1,
        grid=(nb, n_pages // pps),
        in_specs=[per_b((IDX_HEADS * TOK_PAD, IDX_DIM)), per_b((TOK_PAD, LANES)), per_b((PAGE_SIZE, IDX_DIM))]
                 + [page(p) for p in range(pps)],
        out_specs=per_b((n_pages + 1, TOK_PAD, LANES)),
    )
    return pl.pallas_call(
        functools.partial(_sscore_kernel, pps=pps, n_pages=n_pages, t_new=t_new),
        grid_spec=grid_spec,
        out_shape=jax.ShapeDtypeStruct((nb, n_pages + 1, TOK_PAD, LANES), jnp.int32),
        compiler_params=_params(2),
        name="sample_scores",
    )(page_table, qs, misc_bm, kidx_new, *([cache_kidx] * pps))


def _ssel_kernel(keys_ref, bias_ref, *, nch, grp, k_top):
    rows = keys_ref.shape[1]

    def count(pred):
        def body(g, acc):
            for u in range(grp):
                acc = acc + jnp.where(pred(keys_ref[g * grp + u]), 1, 0)
            return acc
        acc = lax.fori_loop(0, nch // grp, body, jnp.zeros((rows, LANES), jnp.int32))
        return jnp.sum(acc, axis=1, keepdims=True)

    thr = _kth_largest(lambda cand: count(lambda kk: kk >= cand), (rows, 1), k_top)
    need = (k_top - count(lambda kk: kk > thr)).astype(F32)
    utri = _tri(lower=False)

    def tie_chunk(c, carry):
        kk = keys_ref[c]
        eq = kk == thr
        eqf = jnp.where(eq, 1.0, 0.0)
        pc = _dot(eqf.astype(BF16), utri) + carry
        sel = (kk > thr) | (eq & (pc <= need) & (thr > KEY_NEG_INF))
        bias_ref[c] = jnp.where(sel, 0.0, NEG_BIG)
        return carry + jnp.sum(eqf, axis=1, keepdims=True)

    lax.fori_loop(0, nch, tie_chunk, jnp.zeros((rows, 1), F32))


def _sample_select(keys, *, k_top):
    nch, rows, _ = keys.shape
    grp = 5 if nch % 5 == 0 else 1
    return pl.pallas_call(
        functools.partial(_ssel_kernel, nch=nch, grp=grp, k_top=k_top),
        grid=(1,),
        in_specs=[pl.BlockSpec(keys.shape, lambda i: (0, 0, 0))],
        out_specs=pl.BlockSpec(keys.shape, lambda i: (0, 0, 0)),
        out_shape=jax.ShapeDtypeStruct(keys.shape, F32),
        compiler_params=_params(1),
        name="sample_select",
    )(keys)


def _sattn_kernel(pt_ref, qbd_ref, bias_ref, knew_ref, vnew_ref, *rest, pps, n_pages):
    k_refs = rest[:pps]
    v_refs = rest[pps:2 * pps]
    o_ref = rest[2 * pps]
    m_scr, l_scr, acc_scr = rest[2 * pps + 1:]
    j = pl.program_id(1)
    nsteps = n_pages // pps
    rows = ATT_HEADS * TOK_PAD

    @pl.when(j == 0)
    def _():
        m_scr[...] = jnp.full((rows, 1), NEG_BIG, F32)
        l_scr[...] = jnp.zeros((rows, 1), F32)
        acc_scr[...] = jnp.zeros((rows, ATT_WIDTH), F32)

    qbd = qbd_ref[...]

    def update(kp, vp, bias):
        s = _dot_nt(qbd, kp) * HEAD_DIM ** -0.5 + jnp.concatenate([bias] * ATT_HEADS, axis=0)
        m = m_scr[...]
        m_new = jnp.maximum(m, jnp.max(s, axis=1, keepdims=True))
        alpha = jnp.exp(m - m_new)
        p = jnp.exp(s - m_new)
        l_scr[...] = l_scr[...] * alpha + jnp.sum(p, axis=1, keepdims=True)
        acc_scr[...] = acc_scr[...] * alpha + _dot(p.astype(BF16), vp)
        m_scr[...] = m_new

    update(jnp.concatenate([r[...] for r in k_refs], axis=0), jnp.concatenate([r[...] for r in v_refs], axis=0),
           jnp.concatenate([bias_ref[j * pps + p] for p in range(pps)], axis=1))

    @pl.when(j == nsteps - 1)
    def _():
        update(knew_ref[...], vnew_ref[...], bias_ref[n_pages])
        full = acc_scr[...] / l_scr[...]
        lane_head = lax.broadcasted_iota(jnp.int32, (TOK_PAD, ATT_WIDTH), 1) // HEAD_DIM
        out = jnp.zeros((TOK_PAD, ATT_WIDTH), F32)
        for h in range(ATT_HEADS):
            out = out + jnp.where(lane_head == h, full[h * TOK_PAD:(h + 1) * TOK_PAD], 0.0)
        o_ref[...] = out


def _sample_attend(page_table, qbd, bias, k_new, v_new, cache_k, cache_v, layer, *, pps):
    nb, n_pages = page_table.shape
    rows = ATT_HEADS * TOK_PAD

    def per_b(shape):
        return pl.BlockSpec((None,) + shape, lambda b, j, pt: (b,) + (0,) * len(shape))

    def page(p):
        return pl.BlockSpec((None, None, PAGE_SIZE, ATT_WIDTH),
                            lambda b, j, pt: (pt[b, j * pps + p], layer, 0, 0))

    grid_spec = pltpu.PrefetchScalarGridSpec(
        num_scalar_prefetch=1,
        grid=(nb, n_pages // pps),
        in_specs=[per_b((rows, ATT_WIDTH)), per_b((n_pages + 1, TOK_PAD, LANES)),
                  per_b((PAGE_SIZE, ATT_WIDTH)), per_b((PAGE_SIZE, ATT_WIDTH))]
                 + [page(p) for p in range(pps)] * 2,
        out_specs=per_b((TOK_PAD, ATT_WIDTH)),
        scratch_shapes=[pltpu.VMEM((rows, 1), F32), pltpu.VMEM((rows, 1), F32), pltpu.VMEM((rows, ATT_WIDTH), F32)],
    )
    return pl.pallas_call(
        functools.partial(_sattn_kernel, pps=pps, n_pages=n_pages),
        grid_spec=grid_spec,
        out_shape=jax.ShapeDtypeStruct((nb, TOK_PAD, ATT_WIDTH), F32),
        compiler_params=_params(2),
        name="sample_attend",
    )(page_table, qbd, bias, k_new, v_new, *([cache_k] * pps), *([cache_v] * pps))


def _post_kernel(x_ref, ys_ref, ya_ref, yc_ref, wo_ref, g1_ref, b1_ref, wu_ref, fcw_ref, fcb_ref, wd_ref,
                 g2_ref, b2_ref, pffn_ref, o_ref, nffn_ref, ubuf, acc_ref, *, tt, rs, nt, cw):
    i = pl.program_id(1)
    h_rows = (FFN_CONV_W - 1) * rs
    hf = _round_up(h_rows, SUBLANES)

    @pl.when(i == 0)
    def _():
        ubuf[hf - h_rows:hf, :] = pffn_ref[...]

    mix = _dot(ys_ref[...].astype(BF16), wo_ref[0:SSD_INNER, :])
    mix = mix + _dot(ya_ref[...].astype(BF16), wo_ref[SSD_INNER:SSD_INNER + ATT_WIDTH, :])
    mix = mix + _dot(yc_ref[...].astype(BF16), wo_ref[SSD_INNER + ATT_WIDTH:, :])
    x1 = _layer_norm(ALPHA * x_ref[...] + mix, g1_ref[...], b1_ref[...])
    x1b = x1.astype(BF16)

    def conv(c0):
        acc = fcb_ref[:, c0:c0 + cw] + fcw_ref[0:1, c0:c0 + cw] * ubuf[hf - h_rows:hf - h_rows + tt, c0:c0 + cw]
        for jj in range(1, FFN_CONV_W):
            off = hf - (FFN_CONV_W - 1 - jj) * rs
            acc = acc + fcw_ref[jj:jj + 1, c0:c0 + cw] * ubuf[off:off + tt, c0:c0 + cw]
        return acc

    for c in range(D_FF // cw):
        cv = c * cw
        cg = D_FF + c * cw
        ubuf[hf:hf + tt, cv:cv + cw] = _dot(x1b, wu_ref[:, cv:cv + cw])
        ubuf[hf:hf + tt, cg:cg + cw] = _dot(x1b, wu_ref[:, cg:cg + cw])
        f = (_silu(conv(cg)) * conv(cv)).astype(BF16)
        part = _dot(f, wd_ref[cv:cv + cw, :])
        if c == 0:
            acc_ref[...] = part
        else:
            acc_ref[...] += part

    o_ref[...] = _layer_norm(ALPHA * x1 + acc_ref[...], g2_ref[...], b2_ref[...])

    @pl.when(i == nt - 1)
    def _():
        nffn_ref[...] = ubuf[hf + tt - h_rows:hf + tt, :]

    ubuf[hf - h_rows:hf, :] = ubuf[hf + tt - h_rows:hf + tt, :]


def _post(x, ys, ya, yc, wl, prev_ffn, *, tt, rs):
    nb, t, _ = x.shape
    nt = t // tt
    hf = _round_up((FFN_CONV_W - 1) * rs, SUBLANES)

    def tile(c):
        return pl.BlockSpec((None, tt, c), lambda b, i: (b, i, 0))

    def const(shape):
        return pl.BlockSpec(shape, lambda b, i: (0,) * len(shape), pipeline_mode=pl.Buffered(1))

    sspec = pl.BlockSpec((None, prev_ffn.shape[1], 2 * D_FF), lambda b, i: (b, 0, 0))
    return pl.pallas_call(
        functools.partial(_post_kernel, tt=tt, rs=rs, nt=nt, cw=256),
        grid=(nb, nt),
        in_specs=[tile(D_MODEL), tile(SSD_INNER), tile(ATT_WIDTH), tile(CC_WIDTH),
                  const((D_MODEL, D_MODEL)), const((1, D_MODEL)), const((1, D_MODEL)),
                  const((D_MODEL, 2 * D_FF)), const((FFN_CONV_W, 2 * D_FF)), const((1, 2 * D_FF)),
                  const((D_FF, D_MODEL)), const((1, D_MODEL)), const((1, D_MODEL)), sspec],
        out_specs=[tile(D_MODEL), sspec],
        out_shape=[jax.ShapeDtypeStruct((nb, t, D_MODEL), F32), jax.ShapeDtypeStruct(prev_ffn.shape, F32)],
        scratch_shapes=[pltpu.VMEM((hf + tt, 2 * D_FF), F32), pltpu.VMEM((tt, D_MODEL), F32)],
        compiler_params=_params(2),
        name="post",
    )(x, ys, ya, yc, wl["w_out"], wl["ln1_g"], wl["ln1_b"], wl["ffn_w_up"], wl["ffn_conv_w"], wl["ffn_conv_b"],
      wl["ffn_w_down"], wl["ln2_g"], wl["ln2_b"], prev_ffn)


def _pack_w_in(w_in):
    z, xbc, dt, q, k, v, qi, ki, wi, glu = jnp.split(w_in, [int(c) for c in np.cumsum(IN_SIZES)[:-1]], axis=-1)
    d = w_in.shape[0]
    misc = jnp.concatenate([ki, wi, jnp.zeros((d, D_MODEL, LANES - IDX_DIM - IDX_HEADS), w_in.dtype)], axis=-1)
    dtp = jnp.concatenate([dt, jnp.zeros((d, D_MODEL, LANES - SSD_HEADS), w_in.dtype)], axis=-1)
    return jnp.concatenate([z, xbc, q, k, v, qi, glu, misc, dtp], axis=-1).astype(BF16)


def _layer_weights(p, l):
    pad_l = LANES - SSD_HEADS
    return {
        "w_in": p["w_in_packed"][l],
        "ssd_conv_w": p["ssd_conv_w"][l], "ssd_conv_b": p["ssd_conv_b"][l][None],
        "cc_conv_w": p["cc_conv_w"][l], "cc_conv_b": p["cc_conv_b"][l][None],
        "cc_ln_g": p["cc_ln_g"][l][None], "cc_ln_b": p["cc_ln_b"][l][None],
        "dt_bias": jnp.pad(p["ssd_dt_bias"][l], (0, pad_l))[None],
        "a_log": jnp.pad(p["ssd_a_log"][l], (0, pad_l))[None],
        "ssd_d": jnp.repeat(p["ssd_d"][l], HEAD_DIM)[None],
        "ssd_norm_g": p["ssd_norm_g"][l][None],
        "w_out": p["w_out_bf"][l], "ln1_g": p["ln1_g"][l][None], "ln1_b": p["ln1_b"][l][None],
        "ffn_w_up": p["ffn_w_up_bf"][l], "ffn_conv_w": p["ffn_conv_w"][l], "ffn_conv_b": p["ffn_conv_b"][l][None],
        "ffn_w_down": p["ffn_w_down_bf"][l], "ln2_g": p["ln2_g"][l][None], "ln2_b": p["ln2_b"][l][None],
    }


def _rope_tables(pos):
    half = HEAD_DIM // 2
    inv = ROPE_THETA ** (-jnp.arange(half, dtype=F32) / half)
    ang = pos.astype(F32)[:, None] * inv[None, :]
    cos = jnp.cos(ang)
    sin = jnp.sin(ang)
    cos_h = jnp.concatenate([cos, cos], axis=-1)
    sin_h = jnp.concatenate([-sin, sin], axis=-1)
    n = pos.shape[0]
    pad1 = jnp.ones((n, LANES - IDX_DIM), F32)
    pad0 = jnp.zeros((n, LANES - IDX_DIM), F32)
    return {"cosq": jnp.tile(cos_h, (1, ATT_HEADS)), "sinq": jnp.tile(sin_h, (1, ATT_HEADS)),
            "cosm": jnp.concatenate([cos_h, pad1], axis=-1), "sinm": jnp.concatenate([sin_h, pad0], axis=-1)}


def _prompt_layer(h, wl, tabs, *, tt_pre, tt_post, tq):
    nb, t, _ = h.shape
    zeros = lambda rows, c: jnp.zeros((nb, rows, c), F32)
    (z, xbc, _, k, v, _, _, kidx, dt, ycc, qt, qit, misct, k_bf, kidx_bf, vt, nssd, ncc) = _premix(
        h, wl, tabs, zeros(SSD_CONV_W - 1, SSD_CONV_DIM), zeros(CC_CONV_W - 1, CC_WIDTH), tt=tt_pre, rs=1)
    yssd, ssm = _ssd(xbc, dt, z, jnp.zeros((nb, SSD_HEADS, HEAD_DIM, SSD_STATE), F32), wl)
    yatt = _prompt_attend(qit, qt, misct, kidx_bf, k_bf, vt, tq=tq)
    h, nffn = _post(h, yssd, yatt, ycc, wl, zeros(FFN_CONV_W - 1, 2 * D_FF), tt=tt_post, rs=1)
    states = (k.reshape(nb, t, ATT_HEADS, HEAD_DIM), v.reshape(nb, t, ATT_HEADS, HEAD_DIM), kidx, ssm,
              nssd, ncc, nffn)
    return h, states


def _to_time_major(state):
    b, r, c = state.shape
    return state.transpose(1, 0, 2).reshape(1, r * b, c)


def _to_batch_major(rows, nb):
    _, n, c = rows.shape
    return rows.reshape(n // nb, nb, c).transpose(1, 0, 2)


def _pad_rows(a, rows):
    return jnp.pad(a, ((0, 0), (0, rows - a.shape[1]), (0, 0)))


def _sample_layer(g, wl, tabs, l, cache_k, cache_v, cache_kidx, page_table, ssm_prev, ssd_prev, cc_prev,
                  ffn_prev, *, nb, t_new, pps):
    (z, xbc, q, k, v, qi, misc, kidx, dt, ycc, _, _, _, _, _, _, nssd, ncc) = _premix(
        g, wl, tabs, _to_time_major(ssd_prev), _to_time_major(cc_prev), tt=nb * t_new, rs=nb)
    bm = lambda a: _to_batch_major(a, nb)
    yssd, ssm = _ssd(_pad_rows(bm(xbc), SSD_CHUNK), _pad_rows(bm(dt), SSD_CHUNK), _pad_rows(bm(z), SSD_CHUNK),
                     ssm_prev, wl, t_valid=t_new)
    yssd = _to_time_major(yssd[:, :t_new])
    qi_bm = _pad_rows(bm(qi), TOK_PAD).reshape(nb, TOK_PAD, IDX_HEADS, IDX_DIM)
    qs = qi_bm.transpose(0, 2, 1, 3).reshape(nb, IDX_HEADS * TOK_PAD, IDX_DIM)
    keys = _sample_scores(page_table, qs, _pad_rows(bm(misc), TOK_PAD), _pad_rows(bm(kidx), PAGE_SIZE),
                          cache_kidx, l, pps=pps, t_new=t_new)
    n_chunks = keys.shape[1]
    n_keys = (n_chunks - 1) * PAGE_SIZE + t_new
    key_rows = keys[:, :, :t_new].transpose(1, 0, 2, 3).reshape(n_chunks, nb * t_new, LANES)
    bias_rows = _sample_select(key_rows, k_top=min(TOPK_MAX, n_keys // 4))
    bias = bias_rows.reshape(n_chunks, nb, t_new, LANES).transpose(1, 0, 2, 3)
    bias = jnp.pad(bias, ((0, 0), (0, 0), (0, TOK_PAD - t_new), (0, 0)))
    q_bm = _pad_rows(bm(q), TOK_PAD).reshape(nb, TOK_PAD, ATT_HEADS, HEAD_DIM)
    eye = jnp.eye(ATT_HEADS, dtype=q_bm.dtype)
    qbd = jnp.einsum("bthd,hg->bhtgd", q_bm, eye).reshape(nb, ATT_HEADS * TOK_PAD, ATT_WIDTH)
    k_bm = bm(k)
    v_bm = bm(v)
    yatt = _sample_attend(page_table, qbd, bias, _pad_rows(k_bm, PAGE_SIZE).astype(BF16),
                          _pad_rows(v_bm, PAGE_SIZE).astype(BF16), cache_k, cache_v, l, pps=pps)
    yatt = _to_time_major(yatt[:, :t_new])
    g, nffn = _post(g, yssd, yatt, ycc, wl, _to_time_major(ffn_prev), tt=nb * t_new, rs=nb)
    states = (k_bm.reshape(nb, t_new, ATT_HEADS, HEAD_DIM), v_bm.reshape(nb, t_new, ATT_HEADS, HEAD_DIM),
              bm(kidx), ssm, bm(nssd), bm(ncc), bm(nffn))
    return g, states


def kernel(x_prompt, x_sample, cache_k, cache_v, cache_kidx, page_table, state_ssm, state_ssd_conv, state_cc_conv, state_ffn_conv, ln0_g, ln0_b, w_in, ssd_conv_w, ssd_conv_b, ssd_dt_bias, ssd_a_log, ssd_d, ssd_norm_g, cc_conv_w, cc_conv_b, cc_ln_g, cc_ln_b, w_out, ln1_g, ln1_b, ffn_w_up, ffn_conv_w, ffn_conv_b, ffn_w_down, ln2_g, ln2_b):
    depth = w_in.shape[0]
    p = {"w_in_packed": _pack_w_in(w_in), "ssd_conv_w": ssd_conv_w, "ssd_conv_b": ssd_conv_b,
         "ssd_dt_bias": ssd_dt_bias, "ssd_a_log": ssd_a_log, "ssd_d": ssd_d, "ssd_norm_g": ssd_norm_g,
         "cc_conv_w": cc_conv_w, "cc_conv_b": cc_conv_b, "cc_ln_g": cc_ln_g, "cc_ln_b": cc_ln_b,
         "w_out_bf": w_out.astype(BF16), "ln1_g": ln1_g, "ln1_b": ln1_b,
         "ffn_w_up_bf": ffn_w_up.astype(BF16), "ffn_conv_w": ffn_conv_w, "ffn_conv_b": ffn_conv_b,
         "ffn_w_down_bf": ffn_w_down.astype(BF16), "ln2_g": ln2_g, "ln2_b": ln2_b}
    layers = [_layer_weights(p, l) for l in range(depth)]

    bp, sp, _ = x_prompt.shape
    tt_pre = min(512, sp)
    h = _ln_rows(x_prompt.reshape(bp * sp, D_MODEL), ln0_g, ln0_b, tt_pre).reshape(bp, sp, D_MODEL)
    tabs_p = _rope_tables(jnp.arange(sp))
    p_states = []
    for l in range(depth):
        h, st = _prompt_layer(h, layers[l], tabs_p, tt_pre=tt_pre, tt_post=min(256, sp), tq=128)
        p_states.append(st)

    nb, ts, _ = x_sample.shape
    n_pages = page_table.shape[1]
    pps = math.gcd(8, n_pages)
    paged = cache_k.shape[:3] + (ATT_WIDTH,)
    cache_k = cache_k.reshape(paged).astype(BF16)
    cache_v = cache_v.reshape(paged).astype(BF16)
    g = _ln_rows(x_sample.transpose(1, 0, 2).reshape(ts * nb, D_MODEL), ln0_g, ln0_b, ts * nb)
    g = g.reshape(1, ts * nb, D_MODEL)
    tabs_s = _rope_tables(PAST_LEN + jnp.arange(ts * nb) // nb)
    s_states = []
    for l in range(depth):
        g, st = _sample_layer(g, layers[l], tabs_s, l, cache_k, cache_v, cache_kidx, page_table,
                              state_ssm[:, l], state_ssd_conv[:, l], state_cc_conv[:, l], state_ffn_conv[:, l],
                              nb=nb, t_new=ts, pps=pps)
        s_states.append(st)
    y_sample = _to_batch_major(g, nb)

    p_out = [jnp.stack(a, axis=1) for a in zip(*p_states)]
    s_out = [jnp.stack(a, axis=1) for a in zip(*s_states)]
    return (h, y_sample, *p_out, *s_out)
```

```python
import functools
import math

import jax
import jax.numpy as jnp
import numpy as np
from jax import lax
from jax.experimental import pallas as pl
from jax.experimental.pallas import tpu as pltpu

D_MODEL = 1024
DEPTH = 4
PAST_LEN = 8192
PAGE_SIZE = 128
HEAD_DIM = 64
SSD_HEADS = 8
SSD_INNER = SSD_HEADS * HEAD_DIM
SSD_GROUPS = 2
SSD_STATE = 64
SSD_CONV_W = 4
SSD_CONV_DIM = SSD_INNER + 2 * SSD_GROUPS * SSD_STATE
SSD_CHUNK = 128
ATT_HEADS = 4
ATT_WIDTH = ATT_HEADS * HEAD_DIM
IDX_HEADS = 4
IDX_DIM = 64
TOPK_MAX = 256
ROPE_THETA = 10000.0
CC_WIDTH = 256
CC_CONV_W = 31
D_FF = 2816
FFN_CONV_W = 3
ALPHA = (2 * DEPTH) ** 0.25
EPS = 1e-5
IN_SIZES = (SSD_INNER, SSD_CONV_DIM, SSD_HEADS, ATT_WIDTH, ATT_WIDTH, ATT_WIDTH,
            IDX_HEADS * IDX_DIM, IDX_DIM, IDX_HEADS, 2 * CC_WIDTH)

LANES = 128
SUBLANES = 8
VMEM_LIMIT = 56 * 1024 * 1024

C_Z = 0
C_XBC = C_Z + SSD_INNER
C_Q = C_XBC + SSD_CONV_DIM
C_K = C_Q + ATT_WIDTH
C_V = C_K + ATT_WIDTH
C_QI = C_V + ATT_WIDTH
C_GLU = C_QI + IDX_HEADS * IDX_DIM
C_MISC = C_GLU + 2 * CC_WIDTH
C_DT = C_MISC + LANES
C_END = C_DT + LANES
WI_LANE = IDX_DIM

NEG_BIG = -1e30
KEY_NEG_INF = np.int32(-2139095041)
INT32_MIN = np.int32(-2 ** 31)
HALF_BITS = 16
HALF_MASK = (1 << HALF_BITS) - 1
HALF_MIN = -(1 << (HALF_BITS - 1))
PACKED_ROWS = 2 * SUBLANES

F32 = jnp.float32
BF16 = jnp.bfloat16
NT_DIMS = (((1,), (1,)), ((), ()))


def _round_up(n, m):
    return (n + m - 1) // m * m


def _silu(x):
    return x * jax.nn.sigmoid(x)


def _layer_norm(x, g, b):
    mu = jnp.mean(x, -1, keepdims=True)
    var = jnp.mean(jnp.square(x - mu), -1, keepdims=True)
    return (x - mu) * lax.rsqrt(var + EPS) * g + b


def _dot(a, b):
    return jnp.dot(a, b, preferred_element_type=F32)


def _dot_nt(a, b):
    return lax.dot_general(a, b, NT_DIMS, preferred_element_type=F32)


def _params(n_axes):
    return pltpu.CompilerParams(dimension_semantics=("arbitrary",) * n_axes, vmem_limit_bytes=VMEM_LIMIT)


def _rope(x, cos, sin_signed):
    n = x.shape[-1]
    lane = lax.broadcasted_iota(jnp.int32, x.shape, 1)
    first_half = (lane % HEAD_DIM) < (HEAD_DIM // 2)
    partner = jnp.where(first_half, pltpu.roll(x, n - HEAD_DIM // 2, 1), pltpu.roll(x, HEAD_DIM // 2, 1))
    return x * cos + partner * sin_signed


def _score_key(score):
    score = jnp.where(score == 0.0, 0.0, score)
    bits = pltpu.bitcast(score, jnp.int32)
    return jnp.where(bits < 0, bits ^ np.int32(0x7FFFFFFF), bits)


def _split3(x):
    hi = x.astype(BF16)
    r1 = x - hi.astype(F32)
    mid = r1.astype(BF16)
    lo = (r1 - mid.astype(F32)).astype(BF16)
    return hi, mid, lo


def _ln_kernel(x_ref, g_ref, b_ref, o_ref):
    o_ref[...] = _layer_norm(x_ref[...], g_ref[...], b_ref[...])


def _ln_rows(x2d, g, b, rows):
    n, d = x2d.shape
    return pl.pallas_call(
        _ln_kernel,
        grid=(n // rows,),
        in_specs=[pl.BlockSpec((rows, d), lambda i: (i, 0)),
                  pl.BlockSpec((1, d), lambda i: (0, 0)),
                  pl.BlockSpec((1, d), lambda i: (0, 0))],
        out_specs=pl.BlockSpec((rows, d), lambda i: (i, 0)),
        out_shape=jax.ShapeDtypeStruct((n, d), F32),
        compiler_params=_params(1),
        name="ln0",
    )(x2d, g.reshape(1, d), b.reshape(1, d))


def _premix_kernel(x_ref, w_ref, scw_ref, scb_ref, ccw_ref, ccb_ref, lng_ref, lnb_ref, dtb_ref,
                   cosq_ref, sinq_ref, cosm_ref, sinm_ref, pssd_ref, pcc_ref,
                   z_ref, xbc_ref, q_ref, k_ref, v_ref, qi_ref, misc_ref, kidx_ref, dt_ref, ycc_ref,
                   qt_ref, qit_ref, misct_ref, kbf_ref, kidxbf_ref, vt_ref,
                   nssd_ref, ncc_ref, sbuf, cbuf, *, tt, rs, nt, rb):
    i = pl.program_id(1)
    hs_rows = (SSD_CONV_W - 1) * rs
    hc_rows = (CC_CONV_W - 1) * rs
    hs = _round_up(hs_rows, SUBLANES)
    hc = _round_up(hc_rows, SUBLANES)

    @pl.when(i == 0)
    def _():
        sbuf[hs - hs_rows:hs, :] = pssd_ref[...]
        cbuf[hc - hc_rows:hc, :] = pcc_ref[...]

    xb = x_ref[...].astype(BF16)

    def seg(a, b):
        return _dot(xb, w_ref[:, a:b])

    z_ref[...] = seg(C_Z, C_XBC)

    sbuf[hs:hs + tt, :] = seg(C_XBC, C_Q)
    for r0 in range(0, tt, rb):
        acc = scb_ref[...] + scw_ref[0:1, :] * sbuf[hs - hs_rows + r0:hs - hs_rows + r0 + rb, :]
        for j in range(1, SSD_CONV_W):
            off = hs - (SSD_CONV_W - 1 - j) * rs + r0
            acc = acc + scw_ref[j:j + 1, :] * sbuf[off:off + rb, :]
        xbc_ref[r0:r0 + rb, :] = _silu(acc)

    @pl.when(i == nt - 1)
    def _():
        nssd_ref[...] = sbuf[hs + tt - hs_rows:hs + tt, :]

    sbuf[hs - hs_rows:hs, :] = sbuf[hs + tt - hs_rows:hs + tt, :]

    dtr = seg(C_DT, C_END) + dtb_ref[...]
    sp = jnp.maximum(dtr, 0.0) + jnp.log1p(jnp.exp(-jnp.abs(dtr)))
    lane = lax.broadcasted_iota(jnp.int32, sp.shape, 1)
    dt_ref[...] = jnp.where(lane < SSD_HEADS, sp, 0.0)

    cosq = cosq_ref[...]
    sinq = sinq_ref[...]
    q = _rope(seg(C_Q, C_K), cosq, sinq)
    q_ref[...] = q.astype(BF16)
    qt_ref[...] = q.T.astype(BF16)
    k = _rope(seg(C_K, C_V), cosq, sinq)
    k_ref[...] = k
    kbf_ref[...] = k.astype(BF16)
    v = seg(C_V, C_QI)
    v_ref[...] = v
    vt_ref[...] = v.T.astype(BF16)
    qi = _rope(seg(C_QI, C_GLU), cosq, sinq)
    qi_ref[...] = qi.astype(BF16)
    qit_ref[...] = qi.T.astype(BF16)
    misc = _rope(seg(C_MISC, C_DT), cosm_ref[...], sinm_ref[...])
    misc_ref[...] = misc
    misct_ref[...] = misc.T
    kidx_ref[...] = misc[:, :IDX_DIM]
    kidxbf_ref[...] = misc[:, :IDX_DIM].astype(BF16)

    glu = seg(C_GLU, C_MISC)
    cbuf[hc:hc + tt, :] = glu[:, :CC_WIDTH] * jax.nn.sigmoid(glu[:, CC_WIDTH:])
    for r0 in range(0, tt, rb):
        acc = ccb_ref[...] + ccw_ref[0:1, :] * cbuf[hc - hc_rows + r0:hc - hc_rows + r0 + rb, :]
        for j in range(1, CC_CONV_W):
            off = hc - (CC_CONV_W - 1 - j) * rs + r0
            acc = acc + ccw_ref[j:j + 1, :] * cbuf[off:off + rb, :]
        ycc_ref[r0:r0 + rb, :] = _silu(_layer_norm(acc, lng_ref[...], lnb_ref[...]))

    @pl.when(i == nt - 1)
    def _():
        ncc_ref[...] = cbuf[hc + tt - hc_rows:hc + tt, :]

    cbuf[hc - hc_rows:hc, :] = cbuf[hc + tt - hc_rows:hc + tt, :]


def _premix(x, wl, tabs, prev_ssd, prev_cc, *, tt, rs):
    nb, t, _ = x.shape
    nt = t // tt
    rb = min(tt, 64)
    hs = _round_up((SSD_CONV_W - 1) * rs, SUBLANES)
    hc = _round_up((CC_CONV_W - 1) * rs, SUBLANES)

    def tile(c):
        return pl.BlockSpec((None, tt, c), lambda b, i: (b, i, 0))

    def const(shape):
        return pl.BlockSpec(shape, lambda b, i: (0,) * len(shape))

    def tab(c):
        return pl.BlockSpec((tt, c), lambda b, i: (i, 0))

    def state(rows, c):
        return pl.BlockSpec((None, rows, c), lambda b, i: (b, 0, 0))

    widths = (SSD_INNER, SSD_CONV_DIM, ATT_WIDTH, ATT_WIDTH, ATT_WIDTH, IDX_HEADS * IDX_DIM,
              LANES, IDX_DIM, LANES, CC_WIDTH)
    dtypes = (F32, F32, BF16, F32, F32, BF16, F32, F32, F32, F32)
    out_shape = [jax.ShapeDtypeStruct((nb, t, c), d) for c, d in zip(widths, dtypes)]
    out_specs = [tile(c) for c in widths]

    def tile_t(c):
        return pl.BlockSpec((None, c, tt), lambda b, i: (b, 0, i))

    out_shape += [jax.ShapeDtypeStruct((nb, ATT_WIDTH, t), BF16),
                  jax.ShapeDtypeStruct((nb, IDX_HEADS * IDX_DIM, t), BF16),
                  jax.ShapeDtypeStruct((nb, LANES, t), F32),
                  jax.ShapeDtypeStruct((nb, t, ATT_WIDTH), BF16),
                  jax.ShapeDtypeStruct((nb, t, IDX_DIM), BF16),
                  jax.ShapeDtypeStruct((nb, nt, ATT_WIDTH, tt), BF16)]
    out_specs += [tile_t(ATT_WIDTH), tile_t(IDX_HEADS * IDX_DIM), tile_t(LANES), tile(ATT_WIDTH), tile(IDX_DIM),
                  pl.BlockSpec((None, None, ATT_WIDTH, tt), lambda b, i: (b, i, 0, 0))]
    out_shape += [jax.ShapeDtypeStruct(prev_ssd.shape, F32), jax.ShapeDtypeStruct(prev_cc.shape, F32)]
    out_specs += [state(prev_ssd.shape[1], SSD_CONV_DIM), state(prev_cc.shape[1], CC_WIDTH)]
    return pl.pallas_call(
        functools.partial(_premix_kernel, tt=tt, rs=rs, nt=nt, rb=rb),
        grid=(nb, nt),
        in_specs=[tile(D_MODEL), const((D_MODEL, C_END)),
                  const((SSD_CONV_W, SSD_CONV_DIM)), const((1, SSD_CONV_DIM)),
                  const((CC_CONV_W, CC_WIDTH)), const((1, CC_WIDTH)), const((1, CC_WIDTH)), const((1, CC_WIDTH)),
                  const((1, LANES)),
                  tab(ATT_WIDTH), tab(ATT_WIDTH), tab(LANES), tab(LANES),
                  state(prev_ssd.shape[1], SSD_CONV_DIM), state(prev_cc.shape[1], CC_WIDTH)],
        out_specs=out_specs,
        out_shape=out_shape,
        scratch_shapes=[pltpu.VMEM((hs + tt, SSD_CONV_DIM), F32), pltpu.VMEM((hc + tt, CC_WIDTH), F32)],
        compiler_params=_params(2),
        name="premix",
    )(x, wl["w_in"], wl["ssd_conv_w"], wl["ssd_conv_b"], wl["cc_conv_w"], wl["cc_conv_b"],
      wl["cc_ln_g"], wl["cc_ln_b"], wl["dt_bias"], tabs["cosq"], tabs["sinq"], tabs["cosm"], tabs["sinm"],
      prev_ssd, prev_cc)


def _ssd_kernel(xbc_ref, dt_ref, z_ref, h0_ref, alog_ref, drow_ref, g_ref, y_ref, hout_ref, h_scr, y_scr,
                *, nt, t_valid):
    i = pl.program_id(1)
    q = SSD_CHUNK

    @pl.when(i == 0)
    def _():
        h_scr[...] = h0_ref[...]

    row = lax.broadcasted_iota(jnp.int32, (q, q), 0)
    col = lax.broadcasted_iota(jnp.int32, (q, q), 1)
    causal = col <= row
    tril = jnp.where(causal, 1.0, 0.0).astype(BF16)

    dt = dt_ref[...]
    if t_valid is not None:
        dt = jnp.where(row < t_valid, dt, 0.0)
    lane1 = lax.broadcasted_iota(jnp.int32, (1, LANES), 1)
    a_row = jnp.where(lane1 < SSD_HEADS, -jnp.exp(alog_ref[...]), 0.0)
    da = dt * a_row
    hi, mid, lo = _split3(da)
    cum = _dot(tril, hi) + _dot(tril, mid) + _dot(tril, lo)
    cum_t = cum.T
    last = cum[q - 1:q, :]
    w_s = jnp.exp(last - cum)
    e_cum = jnp.exp(cum)
    e_last = jnp.exp(last)

    erow = lax.broadcasted_iota(jnp.int32, (LANES, SSD_INNER), 0)
    ecol = lax.broadcasted_iota(jnp.int32, (LANES, SSD_INNER), 1)
    expand = jnp.where(ecol // HEAD_DIM == erow, 1.0, 0.0).astype(BF16)
    dhi, dmid, dlo = _split3(dt)
    dt_wide = _dot(dhi, expand) + _dot(dmid, expand) + _dot(dlo, expand)

    xs = xbc_ref[:, :SSD_INNER]
    xdt = xs * dt_wide
    xdt_t = xdt.T
    gn = SSD_GROUPS * SSD_STATE
    rep = SSD_HEADS // SSD_GROUPS
    for g in range(SSD_GROUPS):
        b_g = xbc_ref[:, SSD_INNER + g * SSD_STATE:SSD_INNER + (g + 1) * SSD_STATE]
        c_g = xbc_ref[:, SSD_INNER + gn + g * SSD_STATE:SSD_INNER + gn + (g + 1) * SSD_STATE]
        c_bf = c_g.astype(BF16)
        cb = _dot_nt(c_bf, b_g.astype(BF16))
        for h in range(g * rep, (g + 1) * rep):
            sl = slice(h * HEAD_DIM, (h + 1) * HEAD_DIM)
            seg = cum[:, h:h + 1] - cum_t[h:h + 1, :]
            decay = jnp.exp(jnp.where(causal, seg, -jnp.inf))
            m = (cb * decay).astype(BF16)
            h_prev = h_scr[h]
            y = _dot(m, xdt[:, sl].astype(BF16))
            y = y + _dot_nt(c_bf, h_prev.astype(BF16)) * e_cum[:, h:h + 1]
            bw = (b_g * w_s[:, h:h + 1]).astype(BF16)
            h_scr[h] = h_prev * e_last[:, h:h + 1] + _dot(xdt_t[sl, :].astype(BF16), bw)
            y_scr[:, sl] = y

    y = (y_scr[...] + drow_ref[...] * xs) * _silu(z_ref[...])
    y_ref[...] = y * lax.rsqrt(jnp.mean(y * y, -1, keepdims=True) + EPS) * g_ref[...]

    @pl.when(i == nt - 1)
    def _():
        hout_ref[...] = h_scr[...]


def _ssd(xbc, dt, z, h0, wl, *, t_valid=None):
    nb, t, _ = xbc.shape
    nt = t // SSD_CHUNK

    def tile(c):
        return pl.BlockSpec((None, SSD_CHUNK, c), lambda b, i: (b, i, 0))

    def const(c):
        return pl.BlockSpec((1, c), lambda b, i: (0, 0))

    hspec = pl.BlockSpec((None, SSD_HEADS, HEAD_DIM, SSD_STATE), lambda b, i: (b, 0, 0, 0))
    return pl.pallas_call(
        functools.partial(_ssd_kernel, nt=nt, t_valid=t_valid),
        grid=(nb, nt),
        in_specs=[tile(SSD_CONV_DIM), tile(LANES), tile(SSD_INNER), hspec,
                  const(LANES), const(SSD_INNER), const(SSD_INNER)],
        out_specs=[tile(SSD_INNER), hspec],
        out_shape=[jax.ShapeDtypeStruct((nb, t, SSD_INNER), F32),
                   jax.ShapeDtypeStruct((nb, SSD_HEADS, HEAD_DIM, SSD_STATE), F32)],
        scratch_shapes=[pltpu.VMEM((SSD_HEADS, HEAD_DIM, SSD_STATE), F32),
                        pltpu.VMEM((SSD_CHUNK, SSD_INNER), F32)],
        compiler_params=_params(2),
        name="ssd",
    )(xbc, dt, z, h0, wl["a_log"], wl["ssd_d"], wl["ssd_norm_g"])


def _kth_largest(count_ge, shape, k_top):
    def body(it, prefix):
        cand = prefix ^ lax.shift_left(jnp.int32(1), 31 - it)
        return jnp.where(count_ge(cand) >= k_top, cand, prefix)
    return lax.fori_loop(0, 32, body, jnp.full(shape, INT32_MIN, jnp.int32))


def _tri(lower):
    r = lax.broadcasted_iota(jnp.int32, (LANES, LANES), 0)
    c = lax.broadcasted_iota(jnp.int32, (LANES, LANES), 1)
    return jnp.where((c <= r) if lower else (r <= c), 1.0, 0.0).astype(BF16)


def _pattn_kernel(qit_ref, qt_ref, misct_ref, kidx_ref, k_ref, vt_ref, o_ref, keys_scr, hi_scr, lo_scr, s_scr,
                  *, tq, ck, k_top):
    i = pl.program_id(1)
    nch = (i * tq + tq + ck - 1) // ck
    sub = ck // LANES

    qit = qit_ref[...] * IDX_DIM ** -0.5
    qi_all = jnp.concatenate([qit[h * IDX_DIM:(h + 1) * IDX_DIM, :] for h in range(IDX_HEADS)], axis=1)
    w_rows = misct_ref[WI_LANE:WI_LANE + SUBLANES, :] * IDX_HEADS ** -0.5
    q_pos = i * tq + lax.broadcasted_iota(jnp.int32, (ck, tq), 1)
    k_row = lax.broadcasted_iota(jnp.int32, (ck, tq), 0)

    def score_chunk(c, carry):
        base = pl.multiple_of(c * ck, ck)
        st = _dot(kidx_ref[pl.ds(base, ck), :], qi_all)
        sc = jnp.zeros((ck, tq), F32)
        for h in range(IDX_HEADS):
            sc = sc + jnp.maximum(st[:, h * tq:(h + 1) * tq], 0.0) * w_rows[h:h + 1, :]
        key = jnp.where(k_row + base <= q_pos, _score_key(sc), KEY_NEG_INF)
        keys_scr[c] = key
        hi_scr[c] = (key >> HALF_BITS).astype(jnp.int16)
        return carry

    lax.fori_loop(0, nch, score_chunk, 0)

    def count(pred):
        def body(c, acc):
            for j in range(sub):
                hit = jnp.where(pred(keys_scr[c, j * LANES:(j + 1) * LANES, :]), 1, 0)
                acc = acc + hit.reshape(LANES // SUBLANES, SUBLANES, tq).sum(axis=0)
            return acc
        acc = lax.fori_loop(0, nch, body, jnp.zeros((SUBLANES, tq), jnp.int32))
        return jnp.sum(acc, axis=0, keepdims=True)

    def half_tile(x):
        return jnp.broadcast_to(x, (PACKED_ROWS, tq)).astype(jnp.int16)

    def count16(scr, pred):
        def body(c, accs):
            accs = list(accs)
            for j in range(ck // PACKED_ROWS):
                hit = jnp.where(pred(scr[c, j * PACKED_ROWS:(j + 1) * PACKED_ROWS, :]), jnp.int16(1), jnp.int16(0))
                accs[j % len(accs)] = accs[j % len(accs)] + hit
            return tuple(accs)
        accs = lax.fori_loop(0, nch, body, tuple(jnp.zeros((PACKED_ROWS, tq), jnp.int16) for _ in range(4)))
        total = (accs[0] + accs[1]) + (accs[2] + accs[3])
        return jnp.sum(total.astype(jnp.int32), axis=0, keepdims=True)

    def kth_half(scr, k_need):
        def body(it, u):
            cand = u | lax.shift_left(jnp.int32(1), HALF_BITS - 1 - it)
            tile = half_tile(cand + HALF_MIN)
            return jnp.where(count16(scr, lambda t: t >= tile) >= k_need, cand, u)
        return lax.fori_loop(0, HALF_BITS, body, jnp.zeros((1, tq), jnp.int32))

    hi = kth_half(hi_scr, k_top) + HALF_MIN
    hi_tile = half_tile(hi)
    k_low = k_top - count16(hi_scr, lambda t: t > hi_tile)

    def low_chunk(c, carry):
        key = keys_scr[c]
        low = (key & HALF_MASK) + HALF_MIN
        lo_scr[c] = jnp.where((key >> HALF_BITS) == hi, low, HALF_MIN).astype(jnp.int16)
        return carry

    lax.fori_loop(0, nch, low_chunk, 0)
    thr = hi * (HALF_MASK + 1) + kth_half(lo_scr, k_low)
    need = k_top - count(lambda kk: kk > thr)
    n_tie = count(lambda kk: kk == thr)
    surplus = jnp.max(jnp.where((n_tie > need) & (thr > KEY_NEG_INF), 1, 0))

    @pl.when(surplus > 0)
    def _():
        ltri = _tri(lower=True)
        need_f = need.astype(F32)

        def tie_chunk(c, carry):
            for j in range(sub):
                kk = keys_scr[c, j * LANES:(j + 1) * LANES, :]
                eq = kk == thr
                eqf = jnp.where(eq, 1.0, 0.0)
                pc = _dot(ltri, eqf.astype(BF16)) + carry
                keys_scr[c, j * LANES:(j + 1) * LANES, :] = jnp.where(eq & (pc > need_f), KEY_NEG_INF, kk)
                carry = carry + jnp.sum(eqf, axis=0, keepdims=True)
            return carry

        lax.fori_loop(0, nch, tie_chunk, jnp.zeros((1, tq), F32))

    thr_sel = jnp.maximum(thr, KEY_NEG_INF + 1)

    qt = qt_ref[...] * HEAD_DIM ** -0.5
    row_head = lax.broadcasted_iota(jnp.int32, (ATT_WIDTH, tq), 0) // HEAD_DIM
    q_bd = jnp.concatenate([jnp.where(row_head == h, qt, 0.0).astype(BF16) for h in range(ATT_HEADS)], axis=1)

    def qk(c):
        return _dot(k_ref[pl.ds(pl.multiple_of(c * ck, ck), ck), :], q_bd)

    s_scr[0] = qk(0)

    def att_chunk(c, st):
        ms, ls, accs = st
        s_next = qk(jnp.minimum(c + 1, nch - 1))
        s_all = s_scr[c % 2]
        s_scr[(c + 1) % 2] = s_next
        bias = jnp.where(keys_scr[c] >= thr_sel, 0.0, NEG_BIG)
        vt = vt_ref[c]
        new_m, new_l, new_acc = [], [], []
        for h in range(ATT_HEADS):
            s = s_all[:, h * tq:(h + 1) * tq] + bias
            m_new = jnp.maximum(ms[h], jnp.max(s, axis=0, keepdims=True))
            alpha = jnp.exp(ms[h] - m_new)
            p = jnp.exp(s - m_new)
            new_m.append(m_new)
            new_l.append(ls[h] * alpha + jnp.sum(p, axis=0, keepdims=True))
            new_acc.append(accs[h] * alpha + _dot(vt[h * HEAD_DIM:(h + 1) * HEAD_DIM, :], p.astype(BF16)))
        return tuple(new_m), tuple(new_l), tuple(new_acc)

    init = (tuple(jnp.full((1, tq), NEG_BIG, F32) for _ in range(ATT_HEADS)),
            tuple(jnp.zeros((1, tq), F32) for _ in range(ATT_HEADS)),
            tuple(jnp.zeros((HEAD_DIM, tq), F32) for _ in range(ATT_HEADS)))
    _, ls, accs = lax.fori_loop(0, nch, att_chunk, init)
    out_t = jnp.concatenate([accs[h] / ls[h] for h in range(ATT_HEADS)], axis=0)
    o_ref[...] = out_t.T


def _prompt_attend(qit, qt, misct, kidx_bf, k_bf, vt, *, tq):
    nb, nchunks, _, ck = vt.shape
    t = nchunks * ck
    k_top = min(TOPK_MAX, t // 4)

    def tile_t(c):
        return pl.BlockSpec((None, c, tq), lambda b, i: (b, 0, i))

    def full(c):
        return pl.BlockSpec((None, t, c), lambda b, i: (b, 0, 0))

    return pl.pallas_call(
        functools.partial(_pattn_kernel, tq=tq, ck=ck, k_top=k_top),
        grid=(nb, t // tq),
        in_specs=[tile_t(IDX_HEADS * IDX_DIM), tile_t(ATT_WIDTH), tile_t(LANES),
                  full(IDX_DIM), full(ATT_WIDTH),
                  pl.BlockSpec((None, nchunks, ATT_WIDTH, ck), lambda b, i: (b, 0, 0, 0))],
        out_specs=pl.BlockSpec((None, tq, ATT_WIDTH), lambda b, i: (b, i, 0)),
        out_shape=jax.ShapeDtypeStruct((nb, t, ATT_WIDTH), F32),
        scratch_shapes=[pltpu.VMEM((nchunks, ck, tq), jnp.int32), pltpu.VMEM((nchunks, ck, tq), jnp.int16),
                        pltpu.VMEM((nchunks, ck, tq), jnp.int16), pltpu.VMEM((2, ck, ATT_HEADS * tq), F32)],
        compiler_params=_params(2),
        name="prompt_attend",
    )(qit, qt, misct, kidx_bf, k_bf, vt)


TOK_PAD = 8


def _sscore_kernel(pt_ref, qs_ref, misc_ref, knew_ref, *rest, pps, n_pages, t_new):
    page_refs = rest[:pps]
    keys_ref = rest[pps]
    j = pl.program_id(1)
    nsteps = n_pages // pps
    qs = qs_ref[...]
    wcol = misc_ref[:, WI_LANE:WI_LANE + IDX_HEADS] * IDX_HEADS ** -0.5

    def score_keys(kt):
        s = _dot(qs, kt.astype(BF16)) * IDX_DIM ** -0.5
        sc = jnp.zeros((TOK_PAD, kt.shape[1]), F32)
        for h in range(IDX_HEADS):
            sc = sc + jnp.maximum(s[h * TOK_PAD:(h + 1) * TOK_PAD], 0.0) * wcol[:, h:h + 1]
        return _score_key(sc)

    keys = score_keys(jnp.concatenate([r[...] for r in page_refs], axis=1))
    for p in range(pps):
        keys_ref[j * pps + p] = keys[:, p * PAGE_SIZE:(p + 1) * PAGE_SIZE]

    @pl.when(j == nsteps - 1)
    def _():
        row = lax.broadcasted_iota(jnp.int32, (TOK_PAD, LANES), 0)
        col = lax.broadcasted_iota(jnp.int32, (TOK_PAD, LANES), 1)
        keys_ref[n_pages] = jnp.where((col <= row) & (col < t_new), score_keys(knew_ref[...]), KEY_NEG_INF)


def _sample_scores(page_table, qs, misc_bm, kidx_new_t, cache_kidx_t, layer, *, pps, t_new):
    nb, n_pages = page_table.shape

    def per_b(shape):
        return pl.BlockSpec((None,) + shape, lambda b, j, pt: (b,) + (0,) * len(shape))

    def page(p):
        return pl.BlockSpec((None, None, IDX_DIM, PAGE_SIZE),
                            lambda b, j, pt: (pt[b, j * pps + p], layer, 0, 0))

    grid_spec = pltpu.PrefetchScalarGridSpec(
        num_scalar_prefetch=1,
        grid=(nb, n_pages // pps),
        in_specs=[per_b((IDX_HEADS * TOK_PAD, IDX_DIM)), per_b((TOK_PAD, LANES)), per_b((IDX_DIM, PAGE_SIZE))]
                 + [page(p) for p in range(pps)],
        out_specs=per_b((n_pages + 1, TOK_PAD, LANES)),
    )
    return pl.pallas_call(
        functools.partial(_sscore_kernel, pps=pps, n_pages=n_pages, t_new=t_new),
        grid_spec=grid_spec,
        out_shape=jax.ShapeDtypeStruct((nb, n_pages + 1, TOK_PAD, LANES), jnp.int32),
        compiler_params=_params(2),
        name="sample_scores",
    )(page_table, qs, misc_bm, kidx_new_t, *([cache_kidx_t] * pps))


def _ssel_kernel(keys_ref, bias_ref, *, nch, grp, k_top):
    rows = keys_ref.shape[1]

    def count(pred):
        def body(g, acc):
            for u in range(grp):
                acc = acc + jnp.where(pred(keys_ref[g * grp + u]), 1, 0)
            return acc
        acc = lax.fori_loop(0, nch // grp, body, jnp.zeros((rows, LANES), jnp.int32))
        return jnp.sum(acc, axis=1, keepdims=True)

    thr = _kth_largest(lambda cand: count(lambda kk: kk >= cand), (rows, 1), k_top)
    need = (k_top - count(lambda kk: kk > thr)).astype(F32)
    utri = _tri(lower=False)

    def tie_chunk(c, carry):
        kk = keys_ref[c]
        eq = kk == thr
        eqf = jnp.where(eq, 1.0, 0.0)
        pc = _dot(eqf.astype(BF16), utri) + carry
        sel = (kk > thr) | (eq & (pc <= need) & (thr > KEY_NEG_INF))
        bias_ref[c] = jnp.where(sel, 0.0, NEG_BIG)
        return carry + jnp.sum(eqf, axis=1, keepdims=True)

    lax.fori_loop(0, nch, tie_chunk, jnp.zeros((rows, 1), F32))


def _sample_select(keys, *, k_top):
    nch, rows, _ = keys.shape
    grp = 5 if nch % 5 == 0 else 1
    return pl.pallas_call(
        functools.partial(_ssel_kernel, nch=nch, grp=grp, k_top=k_top),
        grid=(1,),
        in_specs=[pl.BlockSpec(keys.shape, lambda i: (0, 0, 0))],
        out_specs=pl.BlockSpec(keys.shape, lambda i: (0, 0, 0)),
        out_shape=jax.ShapeDtypeStruct(keys.shape, F32),
        compiler_params=_params(1),
        name="sample_select",
    )(keys)


def _sattn_kernel(pt_ref, qbd_ref, bias_ref, knew_ref, vnew_ref, *rest, pps, n_pages):
    k_refs = rest[:pps]
    v_refs = rest[pps:2 * pps]
    o_ref = rest[2 * pps]
    m_scr, l_scr, acc_scr = rest[2 * pps + 1:]
    j = pl.program_id(1)
    nsteps = n_pages // pps
    rows = ATT_HEADS * TOK_PAD

    @pl.when(j == 0)
    def _():
        m_scr[...] = jnp.full((rows, 1), NEG_BIG, F32)
        l_scr[...] = jnp.zeros((rows, 1), F32)
        acc_scr[...] = jnp.zeros((rows, ATT_WIDTH), F32)

    qbd = qbd_ref[...]

    def page_t(ref):
        return ref[...].reshape(ATT_WIDTH, PAGE_SIZE).astype(BF16)

    def update(kt, vt, bias):
        s = _dot(qbd, kt) * HEAD_DIM ** -0.5 + jnp.concatenate([bias] * ATT_HEADS, axis=0)
        m = m_scr[...]
        m_new = jnp.maximum(m, jnp.max(s, axis=1, keepdims=True))
        alpha = jnp.exp(m - m_new)
        p = jnp.exp(s - m_new)
        l_scr[...] = l_scr[...] * alpha + jnp.sum(p, axis=1, keepdims=True)
        acc_scr[...] = acc_scr[...] * alpha + _dot_nt(p.astype(BF16), vt)
        m_scr[...] = m_new

    update(jnp.concatenate([page_t(r) for r in k_refs], axis=1), jnp.concatenate([page_t(r) for r in v_refs], axis=1),
           jnp.concatenate([bias_ref[j * pps + p] for p in range(pps)], axis=1))

    @pl.when(j == nsteps - 1)
    def _():
        update(knew_ref[...], vnew_ref[...], bias_ref[n_pages])
        full = acc_scr[...] / l_scr[...]
        lane_head = lax.broadcasted_iota(jnp.int32, (TOK_PAD, ATT_WIDTH), 1) // HEAD_DIM
        out = jnp.zeros((TOK_PAD, ATT_WIDTH), F32)
        for h in range(ATT_HEADS):
            out = out + jnp.where(lane_head == h, full[h * TOK_PAD:(h + 1) * TOK_PAD], 0.0)
        o_ref[...] = out


def _sample_attend(page_table, qbd, bias, kt_new, vt_new, cache_kt, cache_vt, layer, *, pps):
    nb, n_pages = page_table.shape
    rows = ATT_HEADS * TOK_PAD

    def per_b(shape):
        return pl.BlockSpec((None,) + shape, lambda b, j, pt: (b,) + (0,) * len(shape))

    def page(p):
        return pl.BlockSpec((None, None, ATT_HEADS, HEAD_DIM, PAGE_SIZE),
                            lambda b, j, pt: (pt[b, j * pps + p], layer, 0, 0, 0))

    grid_spec = pltpu.PrefetchScalarGridSpec(
        num_scalar_prefetch=1,
        grid=(nb, n_pages // pps),
        in_specs=[per_b((rows, ATT_WIDTH)), per_b((n_pages + 1, TOK_PAD, LANES)),
                  per_b((ATT_WIDTH, PAGE_SIZE)), per_b((ATT_WIDTH, PAGE_SIZE))]
                 + [page(p) for p in range(pps)] * 2,
        out_specs=per_b((TOK_PAD, ATT_WIDTH)),
        scratch_shapes=[pltpu.VMEM((rows, 1), F32), pltpu.VMEM((rows, 1), F32), pltpu.VMEM((rows, ATT_WIDTH), F32)],
    )
    return pl.pallas_call(
        functools.partial(_sattn_kernel, pps=pps, n_pages=n_pages),
        grid_spec=grid_spec,
        out_shape=jax.ShapeDtypeStruct((nb, TOK_PAD, ATT_WIDTH), F32),
        compiler_params=_params(2),
        name="sample_attend",
    )(page_table, qbd, bias, kt_new, vt_new, *([cache_kt] * pps), *([cache_vt] * pps))


def _post_kernel(x_ref, ys_ref, ya_ref, yc_ref, wo_ref, g1_ref, b1_ref, wu_ref, fcw_ref, fcb_ref, wd_ref,
                 g2_ref, b2_ref, pffn_ref, o_ref, nffn_ref, halo, ubuf, acc_ref, *, tt, rs, nt, cw):
    i = pl.program_id(1)
    h_rows = (FFN_CONV_W - 1) * rs
    hf = _round_up(h_rows, SUBLANES)

    @pl.when(i == 0)
    def _():
        halo[hf - h_rows:hf, :] = pffn_ref[...]

    mix = _dot(ys_ref[...].astype(BF16), wo_ref[0:SSD_INNER, :])
    mix = mix + _dot(ya_ref[...].astype(BF16), wo_ref[SSD_INNER:SSD_INNER + ATT_WIDTH, :])
    mix = mix + _dot(yc_ref[...].astype(BF16), wo_ref[SSD_INNER + ATT_WIDTH:, :])
    x1 = _layer_norm(ALPHA * x_ref[...] + mix, g1_ref[...], b1_ref[...])
    x1b = x1.astype(BF16)

    def up_proj(src, dst):
        ubuf[hf - h_rows:hf, dst:dst + cw] = halo[hf - h_rows:hf, src:src + cw]
        ubuf[hf:hf + tt, dst:dst + cw] = _dot(x1b, wu_ref[:, src:src + cw])
        halo[hf - h_rows:hf, src:src + cw] = ubuf[hf + tt - h_rows:hf + tt, dst:dst + cw]

    def conv(src, dst):
        acc = fcb_ref[:, src:src + cw] + fcw_ref[0:1, src:src + cw] * ubuf[hf - h_rows:hf - h_rows + tt, dst:dst + cw]
        for jj in range(1, FFN_CONV_W):
            off = hf - (FFN_CONV_W - 1 - jj) * rs
            acc = acc + fcw_ref[jj:jj + 1, src:src + cw] * ubuf[off:off + tt, dst:dst + cw]
        return acc

    for c in range(D_FF // cw):
        cv = c * cw
        cg = D_FF + c * cw
        up_proj(cv, 0)
        up_proj(cg, cw)
        f = (_silu(conv(cg, cw)) * conv(cv, 0)).astype(BF16)
        part = _dot(f, wd_ref[cv:cv + cw, :])
        if c == 0:
            acc_ref[...] = part
        else:
            acc_ref[...] += part

    o_ref[...] = _layer_norm(ALPHA * x1 + acc_ref[...], g2_ref[...], b2_ref[...])

    @pl.when(i == nt - 1)
    def _():
        nffn_ref[...] = halo[hf - h_rows:hf, :]


def _post(x, ys, ya, yc, wl, prev_ffn, *, tt, rs):
    nb, t, _ = x.shape
    nt = t // tt
    hf = _round_up((FFN_CONV_W - 1) * rs, SUBLANES)

    def tile(c):
        return pl.BlockSpec((None, tt, c), lambda b, i: (b, i, 0))

    def const(shape):
        return pl.BlockSpec(shape, lambda b, i: (0,) * len(shape), pipeline_mode=pl.Buffered(1))

    sspec = pl.BlockSpec((None, prev_ffn.shape[1], 2 * D_FF), lambda b, i: (b, 0, 0))
    cw = D_FF // 2
    return pl.pallas_call(
        functools.partial(_post_kernel, tt=tt, rs=rs, nt=nt, cw=cw),
        grid=(nb, nt),
        in_specs=[tile(D_MODEL), tile(SSD_INNER), tile(ATT_WIDTH), tile(CC_WIDTH),
                  const((D_MODEL, D_MODEL)), const((1, D_MODEL)), const((1, D_MODEL)),
                  const((D_MODEL, 2 * D_FF)), const((FFN_CONV_W, 2 * D_FF)), const((1, 2 * D_FF)),
                  const((D_FF, D_MODEL)), const((1, D_MODEL)), const((1, D_MODEL)), sspec],
        out_specs=[tile(D_MODEL), sspec],
        out_shape=[jax.ShapeDtypeStruct((nb, t, D_MODEL), F32), jax.ShapeDtypeStruct(prev_ffn.shape, F32)],
        scratch_shapes=[pltpu.VMEM((hf, 2 * D_FF), F32), pltpu.VMEM((hf + tt, 2 * cw), F32),
                        pltpu.VMEM((tt, D_MODEL), F32)],
        compiler_params=_params(2),
        name="post",
    )(x, ys, ya, yc, wl["w_out"], wl["ln1_g"], wl["ln1_b"], wl["ffn_w_up"], wl["ffn_conv_w"], wl["ffn_conv_b"],
      wl["ffn_w_down"], wl["ln2_g"], wl["ln2_b"], prev_ffn)


def _pack_w_in(w_in):
    z, xbc, dt, q, k, v, qi, ki, wi, glu = jnp.split(w_in, [int(c) for c in np.cumsum(IN_SIZES)[:-1]], axis=-1)
    d = w_in.shape[0]
    misc = jnp.concatenate([ki, wi, jnp.zeros((d, D_MODEL, LANES - IDX_DIM - IDX_HEADS), w_in.dtype)], axis=-1)
    dtp = jnp.concatenate([dt, jnp.zeros((d, D_MODEL, LANES - SSD_HEADS), w_in.dtype)], axis=-1)
    return jnp.concatenate([z, xbc, q, k, v, qi, glu, misc, dtp], axis=-1).astype(BF16)


def _layer_weights(p, l):
    pad_l = LANES - SSD_HEADS
    return {
        "w_in": p["w_in_packed"][l],
        "ssd_conv_w": p["ssd_conv_w"][l], "ssd_conv_b": p["ssd_conv_b"][l][None],
        "cc_conv_w": p["cc_conv_w"][l], "cc_conv_b": p["cc_conv_b"][l][None],
        "cc_ln_g": p["cc_ln_g"][l][None], "cc_ln_b": p["cc_ln_b"][l][None],
        "dt_bias": jnp.pad(p["ssd_dt_bias"][l], (0, pad_l))[None],
        "a_log": jnp.pad(p["ssd_a_log"][l], (0, pad_l))[None],
        "ssd_d": jnp.repeat(p["ssd_d"][l], HEAD_DIM)[None],
        "ssd_norm_g": p["ssd_norm_g"][l][None],
        "w_out": p["w_out_bf"][l], "ln1_g": p["ln1_g"][l][None], "ln1_b": p["ln1_b"][l][None],
        "ffn_w_up": p["ffn_w_up_bf"][l], "ffn_conv_w": p["ffn_conv_w"][l], "ffn_conv_b": p["ffn_conv_b"][l][None],
        "ffn_w_down": p["ffn_w_down_bf"][l], "ln2_g": p["ln2_g"][l][None], "ln2_b": p["ln2_b"][l][None],
    }


def _rope_tables(pos):
    half = HEAD_DIM // 2
    inv = ROPE_THETA ** (-jnp.arange(half, dtype=F32) / half)
    ang = pos.astype(F32)[:, None] * inv[None, :]
    cos = jnp.cos(ang)
    sin = jnp.sin(ang)
    cos_h = jnp.concatenate([cos, cos], axis=-1)
    sin_h = jnp.concatenate([-sin, sin], axis=-1)
    n = pos.shape[0]
    pad1 = jnp.ones((n, LANES - IDX_DIM), F32)
    pad0 = jnp.zeros((n, LANES - IDX_DIM), F32)
    return {"cosq": jnp.tile(cos_h, (1, ATT_HEADS)), "sinq": jnp.tile(sin_h, (1, ATT_HEADS)),
            "cosm": jnp.concatenate([cos_h, pad1], axis=-1), "sinm": jnp.concatenate([sin_h, pad0], axis=-1)}


def _prompt_layer(h, wl, tabs, *, tt_pre, tt_post, tq):
    nb, t, _ = h.shape
    zeros = lambda rows, c: jnp.zeros((nb, rows, c), F32)
    (z, xbc, _, k, v, _, _, kidx, dt, ycc, qt, qit, misct, k_bf, kidx_bf, vt, nssd, ncc) = _premix(
        h, wl, tabs, zeros(SSD_CONV_W - 1, SSD_CONV_DIM), zeros(CC_CONV_W - 1, CC_WIDTH), tt=tt_pre, rs=1)
    yssd, ssm = _ssd(xbc, dt, z, jnp.zeros((nb, SSD_HEADS, HEAD_DIM, SSD_STATE), F32), wl)
    yatt = _prompt_attend(qit, qt, misct, kidx_bf, k_bf, vt, tq=tq)
    h, nffn = _post(h, yssd, yatt, ycc, wl, zeros(FFN_CONV_W - 1, 2 * D_FF), tt=tt_post, rs=1)
    states = (k.reshape(nb, t, ATT_HEADS, HEAD_DIM), v.reshape(nb, t, ATT_HEADS, HEAD_DIM), kidx, ssm,
              nssd, ncc, nffn)
    return h, states


def _to_time_major(state):
    b, r, c = state.shape
    return state.transpose(1, 0, 2).reshape(1, r * b, c)


def _to_batch_major(rows, nb):
    _, n, c = rows.shape
    return rows.reshape(n // nb, nb, c).transpose(1, 0, 2)


def _pad_rows(a, rows):
    return jnp.pad(a, ((0, 0), (0, rows - a.shape[1]), (0, 0)))


def _sample_layer(g, wl, tabs, l, cache_k, cache_v, cache_kidx, page_table, ssm_prev, ssd_prev, cc_prev,
                  ffn_prev, *, nb, t_new, pps):
    (z, xbc, q, k, v, qi, misc, kidx, dt, ycc, _, _, _, _, _, _, nssd, ncc) = _premix(
        g, wl, tabs, _to_time_major(ssd_prev), _to_time_major(cc_prev), tt=nb * t_new, rs=nb)
    bm = lambda a: _to_batch_major(a, nb)
    yssd, ssm = _ssd(_pad_rows(bm(xbc), SSD_CHUNK), _pad_rows(bm(dt), SSD_CHUNK), _pad_rows(bm(z), SSD_CHUNK),
                     ssm_prev, wl, t_valid=t_new)
    yssd = _to_time_major(yssd[:, :t_new])
    qi_bm = _pad_rows(bm(qi), TOK_PAD).reshape(nb, TOK_PAD, IDX_HEADS, IDX_DIM)
    qs = qi_bm.transpose(0, 2, 1, 3).reshape(nb, IDX_HEADS * TOK_PAD, IDX_DIM)
    def new_t(a):
        return _pad_rows(a, PAGE_SIZE).transpose(0, 2, 1)

    keys = _sample_scores(page_table, qs, _pad_rows(bm(misc), TOK_PAD), new_t(bm(kidx)),
                          cache_kidx, l, pps=pps, t_new=t_new)
    n_chunks = keys.shape[1]
    n_keys = (n_chunks - 1) * PAGE_SIZE + t_new
    key_rows = keys[:, :, :t_new].transpose(1, 0, 2, 3).reshape(n_chunks, nb * t_new, LANES)
    bias_rows = _sample_select(key_rows, k_top=min(TOPK_MAX, n_keys // 4))
    bias = bias_rows.reshape(n_chunks, nb, t_new, LANES).transpose(1, 0, 2, 3)
    bias = jnp.pad(bias, ((0, 0), (0, 0), (0, TOK_PAD - t_new), (0, 0)))
    q_bm = _pad_rows(bm(q), TOK_PAD).reshape(nb, TOK_PAD, ATT_HEADS, HEAD_DIM)
    eye = jnp.eye(ATT_HEADS, dtype=q_bm.dtype)
    qbd = jnp.einsum("bthd,hg->bhtgd", q_bm, eye).reshape(nb, ATT_HEADS * TOK_PAD, ATT_WIDTH)
    k_bm = bm(k)
    v_bm = bm(v)
    yatt = _sample_attend(page_table, qbd, bias, new_t(k_bm).astype(BF16), new_t(v_bm).astype(BF16),
                          cache_k, cache_v, l, pps=pps)
    yatt = _to_time_major(yatt[:, :t_new])
    g, nffn = _post(g, yssd, yatt, ycc, wl, _to_time_major(ffn_prev), tt=nb * t_new, rs=nb)
    states = (k_bm.reshape(nb, t_new, ATT_HEADS, HEAD_DIM), v_bm.reshape(nb, t_new, ATT_HEADS, HEAD_DIM),
              bm(kidx), ssm, bm(nssd), bm(ncc), bm(nffn))
    return g, states


def kernel(x_prompt, x_sample, cache_k, cache_v, cache_kidx, page_table, state_ssm, state_ssd_conv, state_cc_conv, state_ffn_conv, ln0_g, ln0_b, w_in, ssd_conv_w, ssd_conv_b, ssd_dt_bias, ssd_a_log, ssd_d, ssd_norm_g, cc_conv_w, cc_conv_b, cc_ln_g, cc_ln_b, w_out, ln1_g, ln1_b, ffn_w_up, ffn_conv_w, ffn_conv_b, ffn_w_down, ln2_g, ln2_b):
    depth = w_in.shape[0]
    p = {"w_in_packed": _pack_w_in(w_in), "ssd_conv_w": ssd_conv_w, "ssd_conv_b": ssd_conv_b,
         "ssd_dt_bias": ssd_dt_bias, "ssd_a_log": ssd_a_log, "ssd_d": ssd_d, "ssd_norm_g": ssd_norm_g,
         "cc_conv_w": cc_conv_w, "cc_conv_b": cc_conv_b, "cc_ln_g": cc_ln_g, "cc_ln_b": cc_ln_b,
         "w_out_bf": w_out.astype(BF16), "ln1_g": ln1_g, "ln1_b": ln1_b,
         "ffn_w_up_bf": ffn_w_up.astype(BF16), "ffn_conv_w": ffn_conv_w, "ffn_conv_b": ffn_conv_b,
         "ffn_w_down_bf": ffn_w_down.astype(BF16), "ln2_g": ln2_g, "ln2_b": ln2_b}
    layers = [_layer_weights(p, l) for l in range(depth)]

    bp, sp, _ = x_prompt.shape
    tt_pre = min(512, sp)
    h = _ln_rows(x_prompt.reshape(bp * sp, D_MODEL), ln0_g, ln0_b, tt_pre).reshape(bp, sp, D_MODEL)
    tabs_p = _rope_tables(jnp.arange(sp))
    p_states = []
    for l in range(depth):
        h, st = _prompt_layer(h, layers[l], tabs_p, tt_pre=tt_pre, tt_post=min(512, sp), tq=128)
        p_states.append(st)

    nb, ts, _ = x_sample.shape
    n_pages = page_table.shape[1]
    pps = math.gcd(8, n_pages)
    cache_k = cache_k.transpose(0, 1, 3, 4, 2)
    cache_v = cache_v.transpose(0, 1, 3, 4, 2)
    cache_kidx = cache_kidx.transpose(0, 1, 3, 2)
    g = _ln_rows(x_sample.transpose(1, 0, 2).reshape(ts * nb, D_MODEL), ln0_g, ln0_b, ts * nb)
    g = g.reshape(1, ts * nb, D_MODEL)
    tabs_s = _rope_tables(PAST_LEN + jnp.arange(ts * nb) // nb)
    s_states = []
    for l in range(depth):
        g, st = _sample_layer(g, layers[l], tabs_s, l, cache_k, cache_v, cache_kidx, page_table,
                              state_ssm[:, l], state_ssd_conv[:, l], state_cc_conv[:, l], state_ffn_conv[:, l],
                              nb=nb, t_new=ts, pps=pps)
        s_states.append(st)
    y_sample = _to_batch_major(g, nb)

    p_out = [jnp.stack(a, axis=1) for a in zip(*p_states)]
    s_out = [jnp.stack(a, axis=1) for a in zip(*s_states)]
    return (h, y_sample, *p_out, *s_out)
```

```python
import functools
import math

import jax
import jax.numpy as jnp
import numpy as np
from jax import lax
from jax.experimental import pallas as pl
from jax.experimental.pallas import tpu as pltpu

D_MODEL = 1024
DEPTH = 4
PAST_LEN = 8192
PAGE_SIZE = 128
HEAD_DIM = 64
SSD_HEADS = 8
SSD_INNER = SSD_HEADS * HEAD_DIM
SSD_GROUPS = 2
SSD_STATE = 64
SSD_CONV_W = 4
SSD_CONV_DIM = SSD_INNER + 2 * SSD_GROUPS * SSD_STATE
SSD_CHUNK = 128
ATT_HEADS = 4
ATT_WIDTH = ATT_HEADS * HEAD_DIM
IDX_HEADS = 4
IDX_DIM = 64
TOPK_MAX = 256
ROPE_THETA = 10000.0
CC_WIDTH = 256
CC_CONV_W = 31
D_FF = 2816
FFN_CONV_W = 3
ALPHA = (2 * DEPTH) ** 0.25
EPS = 1e-5
IN_SIZES = (SSD_INNER, SSD_CONV_DIM, SSD_HEADS, ATT_WIDTH, ATT_WIDTH, ATT_WIDTH,
            IDX_HEADS * IDX_DIM, IDX_DIM, IDX_HEADS, 2 * CC_WIDTH)

LANES = 128
SUBLANES = 8
VMEM_LIMIT = 56 * 1024 * 1024

C_Z = 0
C_XBC = C_Z + SSD_INNER
C_Q = C_XBC + SSD_CONV_DIM
C_K = C_Q + ATT_WIDTH
C_V = C_K + ATT_WIDTH
C_QI = C_V + ATT_WIDTH
C_GLU = C_QI + IDX_HEADS * IDX_DIM
C_MISC = C_GLU + 2 * CC_WIDTH
C_DT = C_MISC + LANES
C_END = C_DT + LANES
WI_LANE = IDX_DIM

NEG_BIG = -1e30
KEY_NEG_INF = np.int32(-2139095041)
INT32_MIN = np.int32(-2 ** 31)
HALF_BITS = 16
HALF_MASK = (1 << HALF_BITS) - 1
HALF_MIN = -(1 << (HALF_BITS - 1))
PACKED_ROWS = 2 * SUBLANES

F32 = jnp.float32
BF16 = jnp.bfloat16
NT_DIMS = (((1,), (1,)), ((), ()))


def _round_up(n, m):
    return (n + m - 1) // m * m


def _silu(x):
    return x * jax.nn.sigmoid(x)


def _layer_norm(x, g, b):
    mu = jnp.mean(x, -1, keepdims=True)
    var = jnp.mean(jnp.square(x - mu), -1, keepdims=True)
    return (x - mu) * lax.rsqrt(var + EPS) * g + b


def _dot(a, b):
    return jnp.dot(a, b, preferred_element_type=F32)


def _dot_nt(a, b):
    return lax.dot_general(a, b, NT_DIMS, preferred_element_type=F32)


def _params(n_axes):
    return pltpu.CompilerParams(dimension_semantics=("arbitrary",) * n_axes, vmem_limit_bytes=VMEM_LIMIT)


def _rope(x, cos, sin_signed):
    n = x.shape[-1]
    lane = lax.broadcasted_iota(jnp.int32, x.shape, 1)
    first_half = (lane % HEAD_DIM) < (HEAD_DIM // 2)
    partner = jnp.where(first_half, pltpu.roll(x, n - HEAD_DIM // 2, 1), pltpu.roll(x, HEAD_DIM // 2, 1))
    return x * cos + partner * sin_signed


def _score_key(score):
    score = jnp.where(score == 0.0, 0.0, score)
    bits = pltpu.bitcast(score, jnp.int32)
    return jnp.where(bits < 0, bits ^ np.int32(0x7FFFFFFF), bits)


def _split3(x):
    hi = x.astype(BF16)
    r1 = x - hi.astype(F32)
    mid = r1.astype(BF16)
    lo = (r1 - mid.astype(F32)).astype(BF16)
    return hi, mid, lo


def _ln_kernel(x_ref, g_ref, b_ref, o_ref):
    o_ref[...] = _layer_norm(x_ref[...], g_ref[...], b_ref[...])


def _ln_rows(x2d, g, b, rows):
    n, d = x2d.shape
    return pl.pallas_call(
        _ln_kernel,
        grid=(n // rows,),
        in_specs=[pl.BlockSpec((rows, d), lambda i: (i, 0)),
                  pl.BlockSpec((1, d), lambda i: (0, 0)),
                  pl.BlockSpec((1, d), lambda i: (0, 0))],
        out_specs=pl.BlockSpec((rows, d), lambda i: (i, 0)),
        out_shape=jax.ShapeDtypeStruct((n, d), F32),
        compiler_params=_params(1),
        name="ln0",
    )(x2d, g.reshape(1, d), b.reshape(1, d))


def _premix_kernel(x_ref, w_ref, scw_ref, scb_ref, ccw_ref, ccb_ref, lng_ref, lnb_ref, dtb_ref,
                   cosq_ref, sinq_ref, cosm_ref, sinm_ref, pssd_ref, pcc_ref,
                   z_ref, xbc_ref, q_ref, k_ref, v_ref, qi_ref, misc_ref, kidx_ref, dt_ref, ycc_ref,
                   qt_ref, qit_ref, misct_ref, kbf_ref, kidxbf_ref, vt_ref,
                   nssd_ref, ncc_ref, sbuf, cbuf, *, tt, rs, nt, rb):
    i = pl.program_id(1)
    hs_rows = (SSD_CONV_W - 1) * rs
    hc_rows = (CC_CONV_W - 1) * rs
    hs = _round_up(hs_rows, SUBLANES)
    hc = _round_up(hc_rows, SUBLANES)

    @pl.when(i == 0)
    def _():
        sbuf[hs - hs_rows:hs, :] = pssd_ref[...]
        cbuf[hc - hc_rows:hc, :] = pcc_ref[...]

    xb = x_ref[...].astype(BF16)

    def seg(a, b):
        return _dot(xb, w_ref[:, a:b])

    z_ref[...] = seg(C_Z, C_XBC)

    sbuf[hs:hs + tt, :] = seg(C_XBC, C_Q)
    for r0 in range(0, tt, rb):
        acc = scb_ref[...] + scw_ref[0:1, :] * sbuf[hs - hs_rows + r0:hs - hs_rows + r0 + rb, :]
        for j in range(1, SSD_CONV_W):
            off = hs - (SSD_CONV_W - 1 - j) * rs + r0
            acc = acc + scw_ref[j:j + 1, :] * sbuf[off:off + rb, :]
        xbc_ref[r0:r0 + rb, :] = _silu(acc)

    @pl.when(i == nt - 1)
    def _():
        nssd_ref[...] = sbuf[hs + tt - hs_rows:hs + tt, :]

    sbuf[hs - hs_rows:hs, :] = sbuf[hs + tt - hs_rows:hs + tt, :]

    dtr = seg(C_DT, C_END) + dtb_ref[...]
    sp = jnp.maximum(dtr, 0.0) + jnp.log1p(jnp.exp(-jnp.abs(dtr)))
    lane = lax.broadcasted_iota(jnp.int32, sp.shape, 1)
    dt_ref[...] = jnp.where(lane < SSD_HEADS, sp, 0.0)

    cosq = cosq_ref[...]
    sinq = sinq_ref[...]
    q = _rope(seg(C_Q, C_K), cosq, sinq)
    q_ref[...] = q.astype(BF16)
    qt_ref[...] = q.T.astype(BF16)
    k = _rope(seg(C_K, C_V), cosq, sinq)
    k_ref[...] = k
    kbf_ref[...] = k.astype(BF16)
    v = seg(C_V, C_QI)
    v_ref[...] = v
    vt_ref[...] = v.T.astype(BF16)
    qi = _rope(seg(C_QI, C_GLU), cosq, sinq)
    qi_ref[...] = qi.astype(BF16)
    qit_ref[...] = qi.T.astype(BF16)
    misc = _rope(seg(C_MISC, C_DT), cosm_ref[...], sinm_ref[...])
    misc_ref[...] = misc
    misct_ref[...] = misc.T
    kidx_ref[...] = misc[:, :IDX_DIM]
    kidxbf_ref[...] = misc[:, :IDX_DIM].astype(BF16)

    glu = seg(C_GLU, C_MISC)
    cbuf[hc:hc + tt, :] = glu[:, :CC_WIDTH] * jax.nn.sigmoid(glu[:, CC_WIDTH:])
    for r0 in range(0, tt, rb):
        acc = ccb_ref[...] + ccw_ref[0:1, :] * cbuf[hc - hc_rows + r0:hc - hc_rows + r0 + rb, :]
        for j in range(1, CC_CONV_W):
            off = hc - (CC_CONV_W - 1 - j) * rs + r0
            acc = acc + ccw_ref[j:j + 1, :] * cbuf[off:off + rb, :]
        ycc_ref[r0:r0 + rb, :] = _silu(_layer_norm(acc, lng_ref[...], lnb_ref[...]))

    @pl.when(i == nt - 1)
    def _():
        ncc_ref[...] = cbuf[hc + tt - hc_rows:hc + tt, :]

    cbuf[hc - hc_rows:hc, :] = cbuf[hc + tt - hc_rows:hc + tt, :]


def _premix(x, wl, tabs, prev_ssd, prev_cc, *, tt, rs):
    nb, t, _ = x.shape
    nt = t // tt
    rb = min(tt, 64)
    hs = _round_up((SSD_CONV_W - 1) * rs, SUBLANES)
    hc = _round_up((CC_CONV_W - 1) * rs, SUBLANES)

    def tile(c):
        return pl.BlockSpec((None, tt, c), lambda b, i: (b, i, 0))

    def const(shape):
        return pl.BlockSpec(shape, lambda b, i: (0,) * len(shape))

    def tab(c):
        return pl.BlockSpec((tt, c), lambda b, i: (i, 0))

    def state(rows, c):
        return pl.BlockSpec((None, rows, c), lambda b, i: (b, 0, 0))

    widths = (SSD_INNER, SSD_CONV_DIM, ATT_WIDTH, ATT_WIDTH, ATT_WIDTH, IDX_HEADS * IDX_DIM,
              LANES, IDX_DIM, LANES, CC_WIDTH)
    dtypes = (F32, F32, BF16, F32, F32, BF16, F32, F32, F32, F32)
    out_shape = [jax.ShapeDtypeStruct((nb, t, c), d) for c, d in zip(widths, dtypes)]
    out_specs = [tile(c) for c in widths]

    def tile_t(c):
        return pl.BlockSpec((None, c, tt), lambda b, i: (b, 0, i))

    out_shape += [jax.ShapeDtypeStruct((nb, ATT_WIDTH, t), BF16),
                  jax.ShapeDtypeStruct((nb, IDX_HEADS * IDX_DIM, t), BF16),
                  jax.ShapeDtypeStruct((nb, LANES, t), F32),
                  jax.ShapeDtypeStruct((nb, t, ATT_WIDTH), BF16),
                  jax.ShapeDtypeStruct((nb, t, IDX_DIM), BF16),
                  jax.ShapeDtypeStruct((nb, nt, ATT_WIDTH, tt), BF16)]
    out_specs += [tile_t(ATT_WIDTH), tile_t(IDX_HEADS * IDX_DIM), tile_t(LANES), tile(ATT_WIDTH), tile(IDX_DIM),
                  pl.BlockSpec((None, None, ATT_WIDTH, tt), lambda b, i: (b, i, 0, 0))]
    out_shape += [jax.ShapeDtypeStruct(prev_ssd.shape, F32), jax.ShapeDtypeStruct(prev_cc.shape, F32)]
    out_specs += [state(prev_ssd.shape[1], SSD_CONV_DIM), state(prev_cc.shape[1], CC_WIDTH)]
    return pl.pallas_call(
        functools.partial(_premix_kernel, tt=tt, rs=rs, nt=nt, rb=rb),
        grid=(nb, nt),
        in_specs=[tile(D_MODEL), const((D_MODEL, C_END)),
                  const((SSD_CONV_W, SSD_CONV_DIM)), const((1, SSD_CONV_DIM)),
                  const((CC_CONV_W, CC_WIDTH)), const((1, CC_WIDTH)), const((1, CC_WIDTH)), const((1, CC_WIDTH)),
                  const((1, LANES)),
                  tab(ATT_WIDTH), tab(ATT_WIDTH), tab(LANES), tab(LANES),
                  state(prev_ssd.shape[1], SSD_CONV_DIM), state(prev_cc.shape[1], CC_WIDTH)],
        out_specs=out_specs,
        out_shape=out_shape,
        scratch_shapes=[pltpu.VMEM((hs + tt, SSD_CONV_DIM), F32), pltpu.VMEM((hc + tt, CC_WIDTH), F32)],
        compiler_params=_params(2),
        name="premix",
    )(x, wl["w_in"], wl["ssd_conv_w"], wl["ssd_conv_b"], wl["cc_conv_w"], wl["cc_conv_b"],
      wl["cc_ln_g"], wl["cc_ln_b"], wl["dt_bias"], tabs["cosq"], tabs["sinq"], tabs["cosm"], tabs["sinm"],
      prev_ssd, prev_cc)


def _ssd_kernel(xbc_ref, dt_ref, z_ref, h0_ref, alog_ref, drow_ref, g_ref, y_ref, hout_ref, h_scr, y_scr,
                *, nt, t_valid):
    i = pl.program_id(1)
    q = SSD_CHUNK

    @pl.when(i == 0)
    def _():
        h_scr[...] = h0_ref[...]

    row = lax.broadcasted_iota(jnp.int32, (q, q), 0)
    col = lax.broadcasted_iota(jnp.int32, (q, q), 1)
    causal = col <= row
    tril = jnp.where(causal, 1.0, 0.0).astype(BF16)

    dt = dt_ref[...]
    if t_valid is not None:
        dt = jnp.where(row < t_valid, dt, 0.0)
    lane1 = lax.broadcasted_iota(jnp.int32, (1, LANES), 1)
    a_row = jnp.where(lane1 < SSD_HEADS, -jnp.exp(alog_ref[...]), 0.0)
    da = dt * a_row
    hi, mid, lo = _split3(da)
    cum = _dot(tril, hi) + _dot(tril, mid) + _dot(tril, lo)
    cum_t = cum.T
    last = cum[q - 1:q, :]
    w_s = jnp.exp(last - cum)
    e_cum = jnp.exp(cum)
    e_last = jnp.exp(last)

    erow = lax.broadcasted_iota(jnp.int32, (LANES, SSD_INNER), 0)
    ecol = lax.broadcasted_iota(jnp.int32, (LANES, SSD_INNER), 1)
    expand = jnp.where(ecol // HEAD_DIM == erow, 1.0, 0.0).astype(BF16)
    dhi, dmid, dlo = _split3(dt)
    dt_wide = _dot(dhi, expand) + _dot(dmid, expand) + _dot(dlo, expand)

    xs = xbc_ref[:, :SSD_INNER]
    xdt = xs * dt_wide
    xdt_t = xdt.T
    gn = SSD_GROUPS * SSD_STATE
    rep = SSD_HEADS // SSD_GROUPS
    for g in range(SSD_GROUPS):
        b_g = xbc_ref[:, SSD_INNER + g * SSD_STATE:SSD_INNER + (g + 1) * SSD_STATE]
        c_g = xbc_ref[:, SSD_INNER + gn + g * SSD_STATE:SSD_INNER + gn + (g + 1) * SSD_STATE]
        c_bf = c_g.astype(BF16)
        cb = _dot_nt(c_bf, b_g.astype(BF16))
        for h in range(g * rep, (g + 1) * rep):
            sl = slice(h * HEAD_DIM, (h + 1) * HEAD_DIM)
            seg = cum[:, h:h + 1] - cum_t[h:h + 1, :]
            decay = jnp.exp(jnp.where(causal, seg, -jnp.inf))
            m = (cb * decay).astype(BF16)
            h_prev = h_scr[h]
            y = _dot(m, xdt[:, sl].astype(BF16))
            y = y + _dot_nt(c_bf, h_prev.astype(BF16)) * e_cum[:, h:h + 1]
            bw = (b_g * w_s[:, h:h + 1]).astype(BF16)
            h_scr[h] = h_prev * e_last[:, h:h + 1] + _dot(xdt_t[sl, :].astype(BF16), bw)
            y_scr[:, sl] = y

    y = (y_scr[...] + drow_ref[...] * xs) * _silu(z_ref[...])
    y_ref[...] = y * lax.rsqrt(jnp.mean(y * y, -1, keepdims=True) + EPS) * g_ref[...]

    @pl.when(i == nt - 1)
    def _():
        hout_ref[...] = h_scr[...]


def _ssd(xbc, dt, z, h0, wl, *, t_valid=None):
    nb, t, _ = xbc.shape
    nt = t // SSD_CHUNK

    def tile(c):
        return pl.BlockSpec((None, SSD_CHUNK, c), lambda b, i: (b, i, 0))

    def const(c):
        return pl.BlockSpec((1, c), lambda b, i: (0, 0))

    hspec = pl.BlockSpec((None, SSD_HEADS, HEAD_DIM, SSD_STATE), lambda b, i: (b, 0, 0, 0))
    return pl.pallas_call(
        functools.partial(_ssd_kernel, nt=nt, t_valid=t_valid),
        grid=(nb, nt),
        in_specs=[tile(SSD_CONV_DIM), tile(LANES), tile(SSD_INNER), hspec,
                  const(LANES), const(SSD_INNER), const(SSD_INNER)],
        out_specs=[tile(SSD_INNER), hspec],
        out_shape=[jax.ShapeDtypeStruct((nb, t, SSD_INNER), F32),
                   jax.ShapeDtypeStruct((nb, SSD_HEADS, HEAD_DIM, SSD_STATE), F32)],
        scratch_shapes=[pltpu.VMEM((SSD_HEADS, HEAD_DIM, SSD_STATE), F32),
                        pltpu.VMEM((SSD_CHUNK, SSD_INNER), F32)],
        compiler_params=_params(2),
        name="ssd",
    )(xbc, dt, z, h0, wl["a_log"], wl["ssd_d"], wl["ssd_norm_g"])


def _kth_largest(count_ge, shape, k_top):
    def body(it, prefix):
        cand = prefix ^ lax.shift_left(jnp.int32(1), 31 - it)
        return jnp.where(count_ge(cand) >= k_top, cand, prefix)
    return lax.fori_loop(0, 32, body, jnp.full(shape, INT32_MIN, jnp.int32))


def _tri(lower):
    r = lax.broadcasted_iota(jnp.int32, (LANES, LANES), 0)
    c = lax.broadcasted_iota(jnp.int32, (LANES, LANES), 1)
    return jnp.where((c <= r) if lower else (r <= c), 1.0, 0.0).astype(BF16)


def _pattn_kernel(qit_ref, qt_ref, misct_ref, kidx_ref, k_ref, vt_ref, o_ref, keys_scr, hi_scr, lo_scr, s_scr,
                  *, tq, ck, k_top):
    i = pl.program_id(1)
    nch = (i * tq + tq + ck - 1) // ck
    sub = ck // LANES

    qit = qit_ref[...] * IDX_DIM ** -0.5
    qi_all = jnp.concatenate([qit[h * IDX_DIM:(h + 1) * IDX_DIM, :] for h in range(IDX_HEADS)], axis=1)
    w_rows = misct_ref[WI_LANE:WI_LANE + SUBLANES, :] * IDX_HEADS ** -0.5
    q_pos = i * tq + lax.broadcasted_iota(jnp.int32, (ck, tq), 1)
    k_row = lax.broadcasted_iota(jnp.int32, (ck, tq), 0)

    def score_chunk(c, carry):
        base = pl.multiple_of(c * ck, ck)
        st = _dot(kidx_ref[pl.ds(base, ck), :], qi_all)
        sc = jnp.zeros((ck, tq), F32)
        for h in range(IDX_HEADS):
            sc = sc + jnp.maximum(st[:, h * tq:(h + 1) * tq], 0.0) * w_rows[h:h + 1, :]
        key = jnp.where(k_row + base <= q_pos, _score_key(sc), KEY_NEG_INF)
        keys_scr[c] = key
        hi_scr[c] = (key >> HALF_BITS).astype(jnp.int16)
        return carry

    lax.fori_loop(0, nch, score_chunk, 0)

    def count(pred):
        def body(c, acc):
            for j in range(sub):
                hit = jnp.where(pred(keys_scr[c, j * LANES:(j + 1) * LANES, :]), 1, 0)
                acc = acc + hit.reshape(LANES // SUBLANES, SUBLANES, tq).sum(axis=0)
            return acc
        acc = lax.fori_loop(0, nch, body, jnp.zeros((SUBLANES, tq), jnp.int32))
        return jnp.sum(acc, axis=0, keepdims=True)

    def half_tile(x):
        return jnp.broadcast_to(x, (PACKED_ROWS, tq)).astype(jnp.int16)

    def count16(scr, pred):
        def body(c, accs):
            accs = list(accs)
            for j in range(ck // PACKED_ROWS):
                hit = jnp.where(pred(scr[c, j * PACKED_ROWS:(j + 1) * PACKED_ROWS, :]), jnp.int16(1), jnp.int16(0))
                accs[j % len(accs)] = accs[j % len(accs)] + hit
            return tuple(accs)
        accs = lax.fori_loop(0, nch, body, tuple(jnp.zeros((PACKED_ROWS, tq), jnp.int16) for _ in range(4)))
        total = (accs[0] + accs[1]) + (accs[2] + accs[3])
        return jnp.sum(total.astype(jnp.int32), axis=0, keepdims=True)

    def kth_half(scr, k_need):
        def body(it, u):
            cand = u | lax.shift_left(jnp.int32(1), HALF_BITS - 1 - it)
            tile = half_tile(cand + HALF_MIN)
            return jnp.where(count16(scr, lambda t: t >= tile) >= k_need, cand, u)
        return lax.fori_loop(0, HALF_BITS, body, jnp.zeros((1, tq), jnp.int32))

    hi = kth_half(hi_scr, k_top) + HALF_MIN
    hi_tile = half_tile(hi)
    k_low = k_top - count16(hi_scr, lambda t: t > hi_tile)

    def low_chunk(c, carry):
        key = keys_scr[c]
        low = (key & HALF_MASK) + HALF_MIN
        lo_scr[c] = jnp.where((key >> HALF_BITS) == hi, low, HALF_MIN).astype(jnp.int16)
        return carry

    lax.fori_loop(0, nch, low_chunk, 0)
    thr = hi * (HALF_MASK + 1) + kth_half(lo_scr, k_low)
    need = k_top - count(lambda kk: kk > thr)
    n_tie = count(lambda kk: kk == thr)
    surplus = jnp.max(jnp.where((n_tie > need) & (thr > KEY_NEG_INF), 1, 0))

    @pl.when(surplus > 0)
    def _():
        ltri = _tri(lower=True)
        need_f = need.astype(F32)

        def tie_chunk(c, carry):
            for j in range(sub):
                kk = keys_scr[c, j * LANES:(j + 1) * LANES, :]
                eq = kk == thr
                eqf = jnp.where(eq, 1.0, 0.0)
                pc = _dot(ltri, eqf.astype(BF16)) + carry
                keys_scr[c, j * LANES:(j + 1) * LANES, :] = jnp.where(eq & (pc > need_f), KEY_NEG_INF, kk)
                carry = carry + jnp.sum(eqf, axis=0, keepdims=True)
            return carry

        lax.fori_loop(0, nch, tie_chunk, jnp.zeros((1, tq), F32))

    thr_sel = jnp.maximum(thr, KEY_NEG_INF + 1)

    qt = qt_ref[...] * HEAD_DIM ** -0.5
    row_head = lax.broadcasted_iota(jnp.int32, (ATT_WIDTH, tq), 0) // HEAD_DIM
    q_bd = jnp.concatenate([jnp.where(row_head == h, qt, 0.0).astype(BF16) for h in range(ATT_HEADS)], axis=1)

    def qk(c):
        return _dot(k_ref[pl.ds(pl.multiple_of(c * ck, ck), ck), :], q_bd)

    s_scr[0] = qk(0)

    def att_chunk(c, st):
        ms, ls, accs = st
        s_next = qk(jnp.minimum(c + 1, nch - 1))
        s_all = s_scr[c % 2]
        s_scr[(c + 1) % 2] = s_next
        bias = jnp.where(keys_scr[c] >= thr_sel, 0.0, NEG_BIG)
        vt = vt_ref[c]
        new_m, new_l, new_acc = [], [], []
        for h in range(ATT_HEADS):
            s = s_all[:, h * tq:(h + 1) * tq] + bias
            m_new = jnp.maximum(ms[h], jnp.max(s, axis=0, keepdims=True))
            alpha = jnp.exp(ms[h] - m_new)
            p = jnp.exp(s - m_new)
            new_m.append(m_new)
            new_l.append(ls[h] * alpha + jnp.sum(p, axis=0, keepdims=True))
            new_acc.append(accs[h] * alpha + _dot(vt[h * HEAD_DIM:(h + 1) * HEAD_DIM, :], p.astype(BF16)))
        return tuple(new_m), tuple(new_l), tuple(new_acc)

    init = (tuple(jnp.full((1, tq), NEG_BIG, F32) for _ in range(ATT_HEADS)),
            tuple(jnp.zeros((1, tq), F32) for _ in range(ATT_HEADS)),
            tuple(jnp.zeros((HEAD_DIM, tq), F32) for _ in range(ATT_HEADS)))
    _, ls, accs = lax.fori_loop(0, nch, att_chunk, init)
    out_t = jnp.concatenate([accs[h] / ls[h] for h in range(ATT_HEADS)], axis=0)
    o_ref[...] = out_t.T


def _prompt_attend(qit, qt, misct, kidx_bf, k_bf, vt, *, tq):
    nb, nchunks, _, ck = vt.shape
    t = nchunks * ck
    k_top = min(TOPK_MAX, t // 4)

    def tile_t(c):
        return pl.BlockSpec((None, c, tq), lambda b, i: (b, 0, i))

    def full(c):
        return pl.BlockSpec((None, t, c), lambda b, i: (b, 0, 0))

    return pl.pallas_call(
        functools.partial(_pattn_kernel, tq=tq, ck=ck, k_top=k_top),
        grid=(nb, t // tq),
        in_specs=[tile_t(IDX_HEADS * IDX_DIM), tile_t(ATT_WIDTH), tile_t(LANES),
                  full(IDX_DIM), full(ATT_WIDTH),
                  pl.BlockSpec((None, nchunks, ATT_WIDTH, ck), lambda b, i: (b, 0, 0, 0))],
        out_specs=pl.BlockSpec((None, tq, ATT_WIDTH), lambda b, i: (b, i, 0)),
        out_shape=jax.ShapeDtypeStruct((nb, t, ATT_WIDTH), F32),
        scratch_shapes=[pltpu.VMEM((nchunks, ck, tq), jnp.int32), pltpu.VMEM((nchunks, ck, tq), jnp.int16),
                        pltpu.VMEM((nchunks, ck, tq), jnp.int16), pltpu.VMEM((2, ck, ATT_HEADS * tq), F32)],
        compiler_params=_params(2),
        name="prompt_attend",
    )(qit, qt, misct, kidx_bf, k_bf, vt)


TOK_PAD = 8


def _sscore_kernel(pt_ref, qs_ref, misc_ref, knew_ref, *rest, pps, n_pages, t_new):
    page_refs = rest[:pps]
    keys_ref = rest[pps]
    j = pl.program_id(1)
    nsteps = n_pages // pps
    qs = qs_ref[...]
    wcol = misc_ref[:, WI_LANE:WI_LANE + IDX_HEADS] * IDX_HEADS ** -0.5

    def score_keys(kt):
        s = _dot(qs, kt.astype(BF16)) * IDX_DIM ** -0.5
        sc = jnp.zeros((TOK_PAD, kt.shape[1]), F32)
        for h in range(IDX_HEADS):
            sc = sc + jnp.maximum(s[h * TOK_PAD:(h + 1) * TOK_PAD], 0.0) * wcol[:, h:h + 1]
        return _score_key(sc)

    keys = score_keys(jnp.concatenate([r[...] for r in page_refs], axis=1))
    for p in range(pps):
        keys_ref[j * pps + p] = keys[:, p * PAGE_SIZE:(p + 1) * PAGE_SIZE]

    @pl.when(j == nsteps - 1)
    def _():
        row = lax.broadcasted_iota(jnp.int32, (TOK_PAD, LANES), 0)
        col = lax.broadcasted_iota(jnp.int32, (TOK_PAD, LANES), 1)
        keys_ref[n_pages] = jnp.where((col <= row) & (col < t_new), score_keys(knew_ref[...]), KEY_NEG_INF)


def _sample_scores(page_table, qs, misc_bm, kidx_new_t, cache_kidx_t, layer, *, pps, t_new):
    nb, n_pages = page_table.shape

    def per_b(shape):
        return pl.BlockSpec((None,) + shape, lambda b, j, pt: (b,) + (0,) * len(shape))

    def page(p):
        return pl.BlockSpec((None, None, IDX_DIM, PAGE_SIZE),
                            lambda b, j, pt: (pt[b, j * pps + p], layer, 0, 0))

    grid_spec = pltpu.PrefetchScalarGridSpec(
        num_scalar_prefetch=1,
        grid=(nb, n_pages // pps),
        in_specs=[per_b((IDX_HEADS * TOK_PAD, IDX_DIM)), per_b((TOK_PAD, LANES)), per_b((IDX_DIM, PAGE_SIZE))]
                 + [page(p) for p in range(pps)],
        out_specs=per_b((n_pages + 1, TOK_PAD, LANES)),
    )
    return pl.pallas_call(
        functools.partial(_sscore_kernel, pps=pps, n_pages=n_pages, t_new=t_new),
        grid_spec=grid_spec,
        out_shape=jax.ShapeDtypeStruct((nb, n_pages + 1, TOK_PAD, LANES), jnp.int32),
        compiler_params=_params(2),
        name="sample_scores",
    )(page_table, qs, misc_bm, kidx_new_t, *([cache_kidx_t] * pps))


def _ssel_kernel(keys_ref, bias_ref, *, nch, grp, k_top):
    rows = keys_ref.shape[1]

    def count(pred):
        def body(g, acc):
            for u in range(grp):
                acc = acc + jnp.where(pred(keys_ref[g * grp + u]), 1, 0)
            return acc
        acc = lax.fori_loop(0, nch // grp, body, jnp.zeros((rows, LANES), jnp.int32))
        return jnp.sum(acc, axis=1, keepdims=True)

    thr = _kth_largest(lambda cand: count(lambda kk: kk >= cand), (rows, 1), k_top)
    need = (k_top - count(lambda kk: kk > thr)).astype(F32)
    utri = _tri(lower=False)

    def tie_chunk(c, carry):
        kk = keys_ref[c]
        eq = kk == thr
        eqf = jnp.where(eq, 1.0, 0.0)
        pc = _dot(eqf.astype(BF16), utri) + carry
        sel = (kk > thr) | (eq & (pc <= need) & (thr > KEY_NEG_INF))
        bias_ref[c] = jnp.where(sel, 0.0, NEG_BIG)
        return carry + jnp.sum(eqf, axis=1, keepdims=True)

    lax.fori_loop(0, nch, tie_chunk, jnp.zeros((rows, 1), F32))


def _sample_select(keys, *, k_top):
    nch, rows, _ = keys.shape
    grp = 5 if nch % 5 == 0 else 1
    return pl.pallas_call(
        functools.partial(_ssel_kernel, nch=nch, grp=grp, k_top=k_top),
        grid=(1,),
        in_specs=[pl.BlockSpec(keys.shape, lambda i: (0, 0, 0))],
        out_specs=pl.BlockSpec(keys.shape, lambda i: (0, 0, 0)),
        out_shape=jax.ShapeDtypeStruct(keys.shape, F32),
        compiler_params=_params(1),
        name="sample_select",
    )(keys)


def _sattn_kernel(pt_ref, qbd_ref, bias_ref, knew_ref, vnew_ref, *rest, pps, n_pages):
    k_refs = rest[:pps]
    v_refs = rest[pps:2 * pps]
    o_ref = rest[2 * pps]
    m_scr, l_scr, acc_scr = rest[2 * pps + 1:]
    j = pl.program_id(1)
    nsteps = n_pages // pps
    rows = ATT_HEADS * TOK_PAD

    @pl.when(j == 0)
    def _():
        m_scr[...] = jnp.full((rows, 1), NEG_BIG, F32)
        l_scr[...] = jnp.zeros((rows, 1), F32)
        acc_scr[...] = jnp.zeros((rows, ATT_WIDTH), F32)

    qbd = qbd_ref[...]

    def page_t(ref):
        return ref[...].reshape(ATT_WIDTH, PAGE_SIZE).astype(BF16)

    def update(kt, vt, bias):
        s = _dot(qbd, kt) * HEAD_DIM ** -0.5 + jnp.concatenate([bias] * ATT_HEADS, axis=0)
        m = m_scr[...]
        m_new = jnp.maximum(m, jnp.max(s, axis=1, keepdims=True))
        alpha = jnp.exp(m - m_new)
        p = jnp.exp(s - m_new)
        l_scr[...] = l_scr[...] * alpha + jnp.sum(p, axis=1, keepdims=True)
        acc_scr[...] = acc_scr[...] * alpha + _dot_nt(p.astype(BF16), vt)
        m_scr[...] = m_new

    update(jnp.concatenate([page_t(r) for r in k_refs], axis=1), jnp.concatenate([page_t(r) for r in v_refs], axis=1),
           jnp.concatenate([bias_ref[j * pps + p] for p in range(pps)], axis=1))

    @pl.when(j == nsteps - 1)
    def _():
        update(knew_ref[...], vnew_ref[...], bias_ref[n_pages])
        full = acc_scr[...] / l_scr[...]
        lane_head = lax.broadcasted_iota(jnp.int32, (TOK_PAD, ATT_WIDTH), 1) // HEAD_DIM
        out = jnp.zeros((TOK_PAD, ATT_WIDTH), F32)
        for h in range(ATT_HEADS):
            out = out + jnp.where(lane_head == h, full[h * TOK_PAD:(h + 1) * TOK_PAD], 0.0)
        o_ref[...] = out


def _sample_attend(page_table, qbd, bias, kt_new, vt_new, cache_kt, cache_vt, layer, *, pps):
    nb, n_pages = page_table.shape
    rows = ATT_HEADS * TOK_PAD

    def per_b(shape):
        return pl.BlockSpec((None,) + shape, lambda b, j, pt: (b,) + (0,) * len(shape))

    def page(p):
        return pl.BlockSpec((None, None, ATT_HEADS, HEAD_DIM, PAGE_SIZE),
                            lambda b, j, pt: (pt[b, j * pps + p], layer, 0, 0, 0))

    grid_spec = pltpu.PrefetchScalarGridSpec(
        num_scalar_prefetch=1,
        grid=(nb, n_pages // pps),
        in_specs=[per_b((rows, ATT_WIDTH)), per_b((n_pages + 1, TOK_PAD, LANES)),
                  per_b((ATT_WIDTH, PAGE_SIZE)), per_b((ATT_WIDTH, PAGE_SIZE))]
                 + [page(p) for p in range(pps)] * 2,
        out_specs=per_b((TOK_PAD, ATT_WIDTH)),
        scratch_shapes=[pltpu.VMEM((rows, 1), F32), pltpu.VMEM((rows, 1), F32), pltpu.VMEM((rows, ATT_WIDTH), F32)],
    )
    return pl.pallas_call(
        functools.partial(_sattn_kernel, pps=pps, n_pages=n_pages),
        grid_spec=grid_spec,
        out_shape=jax.ShapeDtypeStruct((nb, TOK_PAD, ATT_WIDTH), F32),
        compiler_params=_params(2),
        name="sample_attend",
    )(page_table, qbd, bias, kt_new, vt_new, *([cache_kt] * pps), *([cache_vt] * pps))


def _post_kernel(x_ref, ys_ref, ya_ref, yc_ref, wo_ref, g1_ref, b1_ref, wu_ref, fcw_ref, fcb_ref, wd_ref,
                 g2_ref, b2_ref, pffn_ref, o_ref, nffn_ref, halo, ubuf, acc_ref, *, tt, rs, nt, cw):
    i = pl.program_id(1)
    h_rows = (FFN_CONV_W - 1) * rs
    hf = _round_up(h_rows, SUBLANES)

    @pl.when(i == 0)
    def _():
        halo[hf - h_rows:hf, :] = pffn_ref[...]

    mix = _dot(ys_ref[...].astype(BF16), wo_ref[0:SSD_INNER, :])
    mix = mix + _dot(ya_ref[...].astype(BF16), wo_ref[SSD_INNER:SSD_INNER + ATT_WIDTH, :])
    mix = mix + _dot(yc_ref[...].astype(BF16), wo_ref[SSD_INNER + ATT_WIDTH:, :])
    x1 = _layer_norm(ALPHA * x_ref[...] + mix, g1_ref[...], b1_ref[...])
    x1b = x1.astype(BF16)

    def up_proj(src, dst):
        ubuf[hf - h_rows:hf, dst:dst + cw] = halo[hf - h_rows:hf, src:src + cw]
        ubuf[hf:hf + tt, dst:dst + cw] = _dot(x1b, wu_ref[:, src:src + cw])
        halo[hf - h_rows:hf, src:src + cw] = ubuf[hf + tt - h_rows:hf + tt, dst:dst + cw]

    def conv(src, dst):
        acc = fcb_ref[:, src:src + cw] + fcw_ref[0:1, src:src + cw] * ubuf[hf - h_rows:hf - h_rows + tt, dst:dst + cw]
        for jj in range(1, FFN_CONV_W):
            off = hf - (FFN_CONV_W - 1 - jj) * rs
            acc = acc + fcw_ref[jj:jj + 1, src:src + cw] * ubuf[off:off + tt, dst:dst + cw]
        return acc

    for c in range(D_FF // cw):
        cv = c * cw
        cg = D_FF + c * cw
        up_proj(cv, 0)
        up_proj(cg, cw)
        f = (_silu(conv(cg, cw)) * conv(cv, 0)).astype(BF16)
        part = _dot(f, wd_ref[cv:cv + cw, :])
        if c == 0:
            acc_ref[...] = part
        else:
            acc_ref[...] += part

    o_ref[...] = _layer_norm(ALPHA * x1 + acc_ref[...], g2_ref[...], b2_ref[...])

    @pl.when(i == nt - 1)
    def _():
        nffn_ref[...] = halo[hf - h_rows:hf, :]


def _post(x, ys, ya, yc, wl, prev_ffn, *, tt, rs):
    nb, t, _ = x.shape
    nt = t // tt
    hf = _round_up((FFN_CONV_W - 1) * rs, SUBLANES)

    def tile(c):
        return pl.BlockSpec((None, tt, c), lambda b, i: (b, i, 0))

    def const(shape):
        return pl.BlockSpec(shape, lambda b, i: (0,) * len(shape), pipeline_mode=pl.Buffered(1))

    sspec = pl.BlockSpec((None, prev_ffn.shape[1], 2 * D_FF), lambda b, i: (b, 0, 0))
    cw = D_FF // 2
    return pl.pallas_call(
        functools.partial(_post_kernel, tt=tt, rs=rs, nt=nt, cw=cw),
        grid=(nb, nt),
        in_specs=[tile(D_MODEL), tile(SSD_INNER), tile(ATT_WIDTH), tile(CC_WIDTH),
                  const((D_MODEL, D_MODEL)), const((1, D_MODEL)), const((1, D_MODEL)),
                  const((D_MODEL, 2 * D_FF)), const((FFN_CONV_W, 2 * D_FF)), const((1, 2 * D_FF)),
                  const((D_FF, D_MODEL)), const((1, D_MODEL)), const((1, D_MODEL)), sspec],
        out_specs=[tile(D_MODEL), sspec],
        out_shape=[jax.ShapeDtypeStruct((nb, t, D_MODEL), F32), jax.ShapeDtypeStruct(prev_ffn.shape, F32)],
        scratch_shapes=[pltpu.VMEM((hf, 2 * D_FF), F32), pltpu.VMEM((hf + tt, 2 * cw), F32),
                        pltpu.VMEM((tt, D_MODEL), F32)],
        compiler_params=_params(2),
        name="post",
    )(x, ys, ya, yc, wl["w_out"], wl["ln1_g"], wl["ln1_b"], wl["ffn_w_up"], wl["ffn_conv_w"], wl["ffn_conv_b"],
      wl["ffn_w_down"], wl["ln2_g"], wl["ln2_b"], prev_ffn)


def _pack_w_in(w_in):
    z, xbc, dt, q, k, v, qi, ki, wi, glu = jnp.split(w_in, [int(c) for c in np.cumsum(IN_SIZES)[:-1]], axis=-1)
    d = w_in.shape[0]
    misc = jnp.concatenate([ki, wi, jnp.zeros((d, D_MODEL, LANES - IDX_DIM - IDX_HEADS), w_in.dtype)], axis=-1)
    dtp = jnp.concatenate([dt, jnp.zeros((d, D_MODEL, LANES - SSD_HEADS), w_in.dtype)], axis=-1)
    return jnp.concatenate([z, xbc, q, k, v, qi, glu, misc, dtp], axis=-1).astype(BF16)


def _layer_weights(p, l):
    pad_l = LANES - SSD_HEADS
    return {
        "w_in": p["w_in_packed"][l],
        "ssd_conv_w": p["ssd_conv_w"][l], "ssd_conv_b": p["ssd_conv_b"][l][None],
        "cc_conv_w": p["cc_conv_w"][l], "cc_conv_b": p["cc_conv_b"][l][None],
        "cc_ln_g": p["cc_ln_g"][l][None], "cc_ln_b": p["cc_ln_b"][l][None],
        "dt_bias": jnp.pad(p["ssd_dt_bias"][l], (0, pad_l))[None],
        "a_log": jnp.pad(p["ssd_a_log"][l], (0, pad_l))[None],
        "ssd_d": jnp.repeat(p["ssd_d"][l], HEAD_DIM)[None],
        "ssd_norm_g": p["ssd_norm_g"][l][None],
        "w_out": p["w_out_bf"][l], "ln1_g": p["ln1_g"][l][None], "ln1_b": p["ln1_b"][l][None],
        "ffn_w_up": p["ffn_w_up_bf"][l], "ffn_conv_w": p["ffn_conv_w"][l], "ffn_conv_b": p["ffn_conv_b"][l][None],
        "ffn_w_down": p["ffn_w_down_bf"][l], "ln2_g": p["ln2_g"][l][None], "ln2_b": p["ln2_b"][l][None],
    }


def _rope_tables(pos):
    half = HEAD_DIM // 2
    inv = ROPE_THETA ** (-jnp.arange(half, dtype=F32) / half)
    ang = pos.astype(F32)[:, None] * inv[None, :]
    cos = jnp.cos(ang)
    sin = jnp.sin(ang)
    cos_h = jnp.concatenate([cos, cos], axis=-1)
    sin_h = jnp.concatenate([-sin, sin], axis=-1)
    n = pos.shape[0]
    pad1 = jnp.ones((n, LANES - IDX_DIM), F32)
    pad0 = jnp.zeros((n, LANES - IDX_DIM), F32)
    return {"cosq": jnp.tile(cos_h, (1, ATT_HEADS)), "sinq": jnp.tile(sin_h, (1, ATT_HEADS)),
            "cosm": jnp.concatenate([cos_h, pad1], axis=-1), "sinm": jnp.concatenate([sin_h, pad0], axis=-1)}


def _prompt_layer(h, wl, tabs, *, tt_pre, tt_post, tq):
    nb, t, _ = h.shape
    zeros = lambda rows, c: jnp.zeros((nb, rows, c), F32)
    (z, xbc, _, k, v, _, _, kidx, dt, ycc, qt, qit, misct, k_bf, kidx_bf, vt, nssd, ncc) = _premix(
        h, wl, tabs, zeros(SSD_CONV_W - 1, SSD_CONV_DIM), zeros(CC_CONV_W - 1, CC_WIDTH), tt=tt_pre, rs=1)
    yssd, ssm = _ssd(xbc, dt, z, jnp.zeros((nb, SSD_HEADS, HEAD_DIM, SSD_STATE), F32), wl)
    yatt = _prompt_attend(qit, qt, misct, kidx_bf, k_bf, vt, tq=tq)
    h, nffn = _post(h, yssd, yatt, ycc, wl, zeros(FFN_CONV_W - 1, 2 * D_FF), tt=tt_post, rs=1)
    states = (k.reshape(nb, t, ATT_HEADS, HEAD_DIM), v.reshape(nb, t, ATT_HEADS, HEAD_DIM), kidx, ssm,
              nssd, ncc, nffn)
    return h, states


def _to_time_major(state):
    b, r, c = state.shape
    return state.transpose(1, 0, 2).reshape(1, r * b, c)


def _to_batch_major(rows, nb):
    _, n, c = rows.shape
    return rows.reshape(n // nb, nb, c).transpose(1, 0, 2)


def _pad_rows(a, rows):
    return jnp.pad(a, ((0, 0), (0, rows - a.shape[1]), (0, 0)))


def _sample_layer(g, wl, tabs, l, cache_k, cache_v, cache_kidx, page_table, ssm_prev, ssd_prev, cc_prev,
                  ffn_prev, *, nb, t_new, pps):
    (z, xbc, q, k, v, qi, misc, kidx, dt, ycc, _, _, _, _, _, _, nssd, ncc) = _premix(
        g, wl, tabs, _to_time_major(ssd_prev), _to_time_major(cc_prev), tt=nb * t_new, rs=nb)
    bm = lambda a: _to_batch_major(a, nb)
    yssd, ssm = _ssd(_pad_rows(bm(xbc), SSD_CHUNK), _pad_rows(bm(dt), SSD_CHUNK), _pad_rows(bm(z), SSD_CHUNK),
                     ssm_prev, wl, t_valid=t_new)
    yssd = _to_time_major(yssd[:, :t_new])
    qi_bm = _pad_rows(bm(qi), TOK_PAD).reshape(nb, TOK_PAD, IDX_HEADS, IDX_DIM)
    qs = qi_bm.transpose(0, 2, 1, 3).reshape(nb, IDX_HEADS * TOK_PAD, IDX_DIM)
    def new_t(a):
        return _pad_rows(a, PAGE_SIZE).transpose(0, 2, 1)

    keys = _sample_scores(page_table, qs, _pad_rows(bm(misc), TOK_PAD), new_t(bm(kidx)),
                          cache_kidx, l, pps=pps, t_new=t_new)
    n_chunks = keys.shape[1]
    n_keys = (n_chunks - 1) * PAGE_SIZE + t_new
    key_rows = keys[:, :, :t_new].transpose(1, 0, 2, 3).reshape(n_chunks, nb * t_new, LANES)
    bias_rows = _sample_select(key_rows, k_top=min(TOPK_MAX, n_keys // 4))
    bias = bias_rows.reshape(n_chunks, nb, t_new, LANES).transpose(1, 0, 2, 3)
    bias = jnp.pad(bias, ((0, 0), (0, 0), (0, TOK_PAD - t_new), (0, 0)))
    q_bm = _pad_rows(bm(q), TOK_PAD).reshape(nb, TOK_PAD, ATT_HEADS, HEAD_DIM)
    eye = jnp.eye(ATT_HEADS, dtype=q_bm.dtype)
    qbd = jnp.einsum("bthd,hg->bhtgd", q_bm, eye).reshape(nb, ATT_HEADS * TOK_PAD, ATT_WIDTH)
    k_bm = bm(k)
    v_bm = bm(v)
    yatt = _sample_attend(page_table, qbd, bias, new_t(k_bm).astype(BF16), new_t(v_bm).astype(BF16),
                          cache_k, cache_v, l, pps=pps)
    yatt = _to_time_major(yatt[:, :t_new])
    g, nffn = _post(g, yssd, yatt, ycc, wl, _to_time_major(ffn_prev), tt=nb * t_new, rs=nb)
    states = (k_bm.reshape(nb, t_new, ATT_HEADS, HEAD_DIM), v_bm.reshape(nb, t_new, ATT_HEADS, HEAD_DIM),
              bm(kidx), ssm, bm(nssd), bm(ncc), bm(nffn))
    return g, states


def kernel(x_prompt, x_sample, cache_k, cache_v, cache_kidx, page_table, state_ssm, state_ssd_conv, state_cc_conv, state_ffn_conv, ln0_g, ln0_b, w_in, ssd_conv_w, ssd_conv_b, ssd_dt_bias, ssd_a_log, ssd_d, ssd_norm_g, cc_conv_w, cc_conv_b, cc_ln_g, cc_ln_b, w_out, ln1_g, ln1_b, ffn_w_up, ffn_conv_w, ffn_conv_b, ffn_w_down, ln2_g, ln2_b):
    depth = w_in.shape[0]
    p = {"w_in_packed": _pack_w_in(w_in), "ssd_conv_w": ssd_conv_w, "ssd_conv_b": ssd_conv_b,
         "ssd_dt_bias": ssd_dt_bias, "ssd_a_log": ssd_a_log, "ssd_d": ssd_d, "ssd_norm_g": ssd_norm_g,
         "cc_conv_w": cc_conv_w, "cc_conv_b": cc_conv_b, "cc_ln_g": cc_ln_g, "cc_ln_b": cc_ln_b,
         "w_out_bf": w_out.astype(BF16), "ln1_g": ln1_g, "ln1_b": ln1_b,
         "ffn_w_up_bf": ffn_w_up.astype(BF16), "ffn_conv_w": ffn_conv_w, "ffn_conv_b": ffn_conv_b,
         "ffn_w_down_bf": ffn_w_down.astype(BF16), "ln2_g": ln2_g, "ln2_b": ln2_b}
    layers = [_layer_weights(p, l) for l in range(depth)]

    bp, sp, _ = x_prompt.shape
    tt_pre = min(512, sp)
    h = _ln_rows(x_prompt.reshape(bp * sp, D_MODEL), ln0_g, ln0_b, tt_pre).reshape(bp, sp, D_MODEL)
    tabs_p = _rope_tables(jnp.arange(sp))
    p_states = []
    for l in range(depth):
        h, st = _prompt_layer(h, layers[l], tabs_p, tt_pre=tt_pre, tt_post=min(512, sp), tq=min(256, sp))
        p_states.append(st)

    nb, ts, _ = x_sample.shape
    n_pages = page_table.shape[1]
    pps = math.gcd(16, n_pages)
    cache_k = cache_k.transpose(0, 1, 3, 4, 2)
    cache_v = cache_v.transpose(0, 1, 3, 4, 2)
    cache_kidx = cache_kidx.transpose(0, 1, 3, 2)
    g = _ln_rows(x_sample.transpose(1, 0, 2).reshape(ts * nb, D_MODEL), ln0_g, ln0_b, ts * nb)
    g = g.reshape(1, ts * nb, D_MODEL)
    tabs_s = _rope_tables(PAST_LEN + jnp.arange(ts * nb) // nb)
    s_states = []
    for l in range(depth):
        g, st = _sample_layer(g, layers[l], tabs_s, l, cache_k, cache_v, cache_kidx, page_table,
                              state_ssm[:, l], state_ssd_conv[:, l], state_cc_conv[:, l], state_ffn_conv[:, l],
                              nb=nb, t_new=ts, pps=pps)
        s_states.append(st)
    y_sample = _to_batch_major(g, nb)

    p_out = [jnp.stack(a, axis=1) for a in zip(*p_states)]
    s_out = [jnp.stack(a, axis=1) for a in zip(*s_states)]
    return (h, y_sample, *p_out, *s_out)
```

```python
import functools
import math

import jax
import jax.numpy as jnp
import numpy as np
from jax import lax
from jax.experimental import pallas as pl
from jax.experimental.pallas import tpu as pltpu

D_MODEL = 1024
DEPTH = 4
PAST_LEN = 8192
PAGE_SIZE = 128
HEAD_DIM = 64
SSD_HEADS = 8
SSD_INNER = SSD_HEADS * HEAD_DIM
SSD_GROUPS = 2
SSD_STATE = 64
SSD_CONV_W = 4
SSD_CONV_DIM = SSD_INNER + 2 * SSD_GROUPS * SSD_STATE
SSD_CHUNK = 128
ATT_HEADS = 4
ATT_WIDTH = ATT_HEADS * HEAD_DIM
IDX_HEADS = 4
IDX_DIM = 64
TOPK_MAX = 256
ROPE_THETA = 10000.0
CC_WIDTH = 256
CC_CONV_W = 31
D_FF = 2816
FFN_CONV_W = 3
ALPHA = (2 * DEPTH) ** 0.25
EPS = 1e-5
IN_SIZES = (SSD_INNER, SSD_CONV_DIM, SSD_HEADS, ATT_WIDTH, ATT_WIDTH, ATT_WIDTH,
            IDX_HEADS * IDX_DIM, IDX_DIM, IDX_HEADS, 2 * CC_WIDTH)

LANES = 128
SUBLANES = 8
VMEM_LIMIT = 56 * 1024 * 1024

C_Z = 0
C_XBC = C_Z + SSD_INNER
C_Q = C_XBC + SSD_CONV_DIM
C_K = C_Q + ATT_WIDTH
C_V = C_K + ATT_WIDTH
C_QI = C_V + ATT_WIDTH
C_GLU = C_QI + IDX_HEADS * IDX_DIM
C_MISC = C_GLU + 2 * CC_WIDTH
C_DT = C_MISC + LANES
C_END = C_DT + LANES
WI_LANE = IDX_DIM

NEG_BIG = -1e30
KEY_NEG_INF = np.int32(-2139095041)
INT32_MIN = np.int32(-2 ** 31)
HALF_BITS = 16
HALF_MASK = (1 << HALF_BITS) - 1
HALF_MIN = -(1 << (HALF_BITS - 1))
PACKED_ROWS = 2 * SUBLANES

F32 = jnp.float32
BF16 = jnp.bfloat16
NT_DIMS = (((1,), (1,)), ((), ()))


def _round_up(n, m):
    return (n + m - 1) // m * m


def _silu(x):
    return x * jax.nn.sigmoid(x)


def _layer_norm(x, g, b):
    mu = jnp.mean(x, -1, keepdims=True)
    var = jnp.mean(jnp.square(x - mu), -1, keepdims=True)
    return (x - mu) * lax.rsqrt(var + EPS) * g + b


def _dot(a, b):
    return jnp.dot(a, b, preferred_element_type=F32)


def _dot_nt(a, b):
    return lax.dot_general(a, b, NT_DIMS, preferred_element_type=F32)


def _params(n_axes):
    return pltpu.CompilerParams(dimension_semantics=("arbitrary",) * n_axes, vmem_limit_bytes=VMEM_LIMIT)


def _rope(x, cos, sin_signed):
    n = x.shape[-1]
    lane = lax.broadcasted_iota(jnp.int32, x.shape, 1)
    first_half = (lane % HEAD_DIM) < (HEAD_DIM // 2)
    partner = jnp.where(first_half, pltpu.roll(x, n - HEAD_DIM // 2, 1), pltpu.roll(x, HEAD_DIM // 2, 1))
    return x * cos + partner * sin_signed


def _score_key(score):
    score = jnp.where(score == 0.0, 0.0, score)
    bits = pltpu.bitcast(score, jnp.int32)
    return jnp.where(bits < 0, bits ^ np.int32(0x7FFFFFFF), bits)


def _split3(x):
    hi = x.astype(BF16)
    r1 = x - hi.astype(F32)
    mid = r1.astype(BF16)
    lo = (r1 - mid.astype(F32)).astype(BF16)
    return hi, mid, lo


def _ln_kernel(x_ref, g_ref, b_ref, o_ref):
    o_ref[...] = _layer_norm(x_ref[...], g_ref[...], b_ref[...])


def _ln_rows(x2d, g, b, rows):
    n, d = x2d.shape
    return pl.pallas_call(
        _ln_kernel,
        grid=(n // rows,),
        in_specs=[pl.BlockSpec((rows, d), lambda i: (i, 0)),
                  pl.BlockSpec((1, d), lambda i: (0, 0)),
                  pl.BlockSpec((1, d), lambda i: (0, 0))],
        out_specs=pl.BlockSpec((rows, d), lambda i: (i, 0)),
        out_shape=jax.ShapeDtypeStruct((n, d), F32),
        compiler_params=_params(1),
        name="ln0",
    )(x2d, g.reshape(1, d), b.reshape(1, d))


def _premix_kernel(x_ref, w_ref, scw_ref, scb_ref, ccw_ref, ccb_ref, lng_ref, lnb_ref, dtb_ref,
                   cosq_ref, sinq_ref, cosm_ref, sinm_ref, pssd_ref, pcc_ref,
                   z_ref, xbc_ref, q_ref, k_ref, v_ref, qi_ref, misc_ref, kidx_ref, dt_ref, ycc_ref,
                   qt_ref, qit_ref, misct_ref, kbf_ref, kidxbf_ref, vt_ref,
                   nssd_ref, ncc_ref, sbuf, cbuf, *, tt, rs, nt, rb):
    i = pl.program_id(1)
    hs_rows = (SSD_CONV_W - 1) * rs
    hc_rows = (CC_CONV_W - 1) * rs
    hs = _round_up(hs_rows, SUBLANES)
    hc = _round_up(hc_rows, SUBLANES)

    @pl.when(i == 0)
    def _():
        sbuf[hs - hs_rows:hs, :] = pssd_ref[...]
        cbuf[hc - hc_rows:hc, :] = pcc_ref[...]

    xb = x_ref[...].astype(BF16)

    def seg(a, b):
        return _dot(xb, w_ref[:, a:b])

    z_ref[...] = seg(C_Z, C_XBC)

    sbuf[hs:hs + tt, :] = seg(C_XBC, C_Q)
    for r0 in range(0, tt, rb):
        acc = scb_ref[...] + scw_ref[0:1, :] * sbuf[hs - hs_rows + r0:hs - hs_rows + r0 + rb, :]
        for j in range(1, SSD_CONV_W):
            off = hs - (SSD_CONV_W - 1 - j) * rs + r0
            acc = acc + scw_ref[j:j + 1, :] * sbuf[off:off + rb, :]
        xbc_ref[r0:r0 + rb, :] = _silu(acc)

    @pl.when(i == nt - 1)
    def _():
        nssd_ref[...] = sbuf[hs + tt - hs_rows:hs + tt, :]

    sbuf[hs - hs_rows:hs, :] = sbuf[hs + tt - hs_rows:hs + tt, :]

    dtr = seg(C_DT, C_END) + dtb_ref[...]
    sp = jnp.maximum(dtr, 0.0) + jnp.log1p(jnp.exp(-jnp.abs(dtr)))
    lane = lax.broadcasted_iota(jnp.int32, sp.shape, 1)
    dt_ref[...] = jnp.where(lane < SSD_HEADS, sp, 0.0)

    cosq = cosq_ref[...]
    sinq = sinq_ref[...]
    q = _rope(seg(C_Q, C_K), cosq, sinq)
    q_ref[...] = q.astype(BF16)
    qt_ref[...] = q.T.astype(BF16)
    k = _rope(seg(C_K, C_V), cosq, sinq)
    k_ref[...] = k
    kbf_ref[...] = k.astype(BF16)
    v = seg(C_V, C_QI)
    v_ref[...] = v
    vt_ref[...] = v.T.astype(BF16)
    qi = _rope(seg(C_QI, C_GLU), cosq, sinq)
    qi_ref[...] = qi.astype(BF16)
    qit_ref[...] = qi.T.astype(BF16)
    misc = _rope(seg(C_MISC, C_DT), cosm_ref[...], sinm_ref[...])
    misc_ref[...] = misc
    misct_ref[...] = misc.T
    kidx_ref[...] = misc[:, :IDX_DIM]
    kidxbf_ref[...] = misc[:, :IDX_DIM].astype(BF16)

    glu = seg(C_GLU, C_MISC)
    cbuf[hc:hc + tt, :] = glu[:, :CC_WIDTH] * jax.nn.sigmoid(glu[:, CC_WIDTH:])
    for r0 in range(0, tt, rb):
        acc = ccb_ref[...] + ccw_ref[0:1, :] * cbuf[hc - hc_rows + r0:hc - hc_rows + r0 + rb, :]
        for j in range(1, CC_CONV_W):
            off = hc - (CC_CONV_W - 1 - j) * rs + r0
            acc = acc + ccw_ref[j:j + 1, :] * cbuf[off:off + rb, :]
        ycc_ref[r0:r0 + rb, :] = _silu(_layer_norm(acc, lng_ref[...], lnb_ref[...]))

    @pl.when(i == nt - 1)
    def _():
        ncc_ref[...] = cbuf[hc + tt - hc_rows:hc + tt, :]

    cbuf[hc - hc_rows:hc, :] = cbuf[hc + tt - hc_rows:hc + tt, :]


def _premix(x, wl, tabs, prev_ssd, prev_cc, *, tt, rs):
    nb, t, _ = x.shape
    nt = t // tt
    rb = min(tt, 64)
    hs = _round_up((SSD_CONV_W - 1) * rs, SUBLANES)
    hc = _round_up((CC_CONV_W - 1) * rs, SUBLANES)

    def tile(c):
        return pl.BlockSpec((None, tt, c), lambda b, i: (b, i, 0))

    def const(shape):
        return pl.BlockSpec(shape, lambda b, i: (0,) * len(shape))

    def tab(c):
        return pl.BlockSpec((tt, c), lambda b, i: (i, 0))

    def state(rows, c):
        return pl.BlockSpec((None, rows, c), lambda b, i: (b, 0, 0))

    widths = (SSD_INNER, SSD_CONV_DIM, ATT_WIDTH, ATT_WIDTH, ATT_WIDTH, IDX_HEADS * IDX_DIM,
              LANES, IDX_DIM, LANES, CC_WIDTH)
    dtypes = (F32, F32, BF16, F32, F32, BF16, F32, F32, F32, F32)
    out_shape = [jax.ShapeDtypeStruct((nb, t, c), d) for c, d in zip(widths, dtypes)]
    out_specs = [tile(c) for c in widths]

    def tile_t(c):
        return pl.BlockSpec((None, c, tt), lambda b, i: (b, 0, i))

    out_shape += [jax.ShapeDtypeStruct((nb, ATT_WIDTH, t), BF16),
                  jax.ShapeDtypeStruct((nb, IDX_HEADS * IDX_DIM, t), BF16),
                  jax.ShapeDtypeStruct((nb, LANES, t), F32),
                  jax.ShapeDtypeStruct((nb, t, ATT_WIDTH), BF16),
                  jax.ShapeDtypeStruct((nb, t, IDX_DIM), BF16),
                  jax.ShapeDtypeStruct((nb, nt, ATT_WIDTH, tt), BF16)]
    out_specs += [tile_t(ATT_WIDTH), tile_t(IDX_HEADS * IDX_DIM), tile_t(LANES), tile(ATT_WIDTH), tile(IDX_DIM),
                  pl.BlockSpec((None, None, ATT_WIDTH, tt), lambda b, i: (b, i, 0, 0))]
    out_shape += [jax.ShapeDtypeStruct(prev_ssd.shape, F32), jax.ShapeDtypeStruct(prev_cc.shape, F32)]
    out_specs += [state(prev_ssd.shape[1], SSD_CONV_DIM), state(prev_cc.shape[1], CC_WIDTH)]
    return pl.pallas_call(
        functools.partial(_premix_kernel, tt=tt, rs=rs, nt=nt, rb=rb),
        grid=(nb, nt),
        in_specs=[tile(D_MODEL), const((D_MODEL, C_END)),
                  const((SSD_CONV_W, SSD_CONV_DIM)), const((1, SSD_CONV_DIM)),
                  const((CC_CONV_W, CC_WIDTH)), const((1, CC_WIDTH)), const((1, CC_WIDTH)), const((1, CC_WIDTH)),
                  const((1, LANES)),
                  tab(ATT_WIDTH), tab(ATT_WIDTH), tab(LANES), tab(LANES),
                  state(prev_ssd.shape[1], SSD_CONV_DIM), state(prev_cc.shape[1], CC_WIDTH)],
        out_specs=out_specs,
        out_shape=out_shape,
        scratch_shapes=[pltpu.VMEM((hs + tt, SSD_CONV_DIM), F32), pltpu.VMEM((hc + tt, CC_WIDTH), F32)],
        compiler_params=_params(2),
        name="premix",
    )(x, wl["w_in"], wl["ssd_conv_w"], wl["ssd_conv_b"], wl["cc_conv_w"], wl["cc_conv_b"],
      wl["cc_ln_g"], wl["cc_ln_b"], wl["dt_bias"], tabs["cosq"], tabs["sinq"], tabs["cosm"], tabs["sinm"],
      prev_ssd, prev_cc)


def _ssd_kernel(xbc_ref, dt_ref, z_ref, h0_ref, alog_ref, drow_ref, g_ref, y_ref, hout_ref, h_scr, y_scr,
                *, nt, t_valid):
    i = pl.program_id(1)
    q = SSD_CHUNK

    @pl.when(i == 0)
    def _():
        h_scr[...] = h0_ref[...]

    row = lax.broadcasted_iota(jnp.int32, (q, q), 0)
    col = lax.broadcasted_iota(jnp.int32, (q, q), 1)
    causal = col <= row
    tril = jnp.where(causal, 1.0, 0.0).astype(BF16)

    dt = dt_ref[...]
    if t_valid is not None:
        dt = jnp.where(row < t_valid, dt, 0.0)
    lane1 = lax.broadcasted_iota(jnp.int32, (1, LANES), 1)
    a_row = jnp.where(lane1 < SSD_HEADS, -jnp.exp(alog_ref[...]), 0.0)
    da = dt * a_row
    hi, mid, lo = _split3(da)
    cum = _dot(tril, hi) + _dot(tril, mid) + _dot(tril, lo)
    cum_t = cum.T
    last = cum[q - 1:q, :]
    w_s = jnp.exp(last - cum)
    e_cum = jnp.exp(cum)
    e_last = jnp.exp(last)

    erow = lax.broadcasted_iota(jnp.int32, (LANES, SSD_INNER), 0)
    ecol = lax.broadcasted_iota(jnp.int32, (LANES, SSD_INNER), 1)
    expand = jnp.where(ecol // HEAD_DIM == erow, 1.0, 0.0).astype(BF16)
    dhi, dmid, dlo = _split3(dt)
    dt_wide = _dot(dhi, expand) + _dot(dmid, expand) + _dot(dlo, expand)

    xs = xbc_ref[:, :SSD_INNER]
    xdt = xs * dt_wide
    xdt_t = xdt.T
    gn = SSD_GROUPS * SSD_STATE
    rep = SSD_HEADS // SSD_GROUPS
    for g in range(SSD_GROUPS):
        b_g = xbc_ref[:, SSD_INNER + g * SSD_STATE:SSD_INNER + (g + 1) * SSD_STATE]
        c_g = xbc_ref[:, SSD_INNER + gn + g * SSD_STATE:SSD_INNER + gn + (g + 1) * SSD_STATE]
        c_bf = c_g.astype(BF16)
        cb = _dot_nt(c_bf, b_g.astype(BF16))
        for h in range(g * rep, (g + 1) * rep):
            sl = slice(h * HEAD_DIM, (h + 1) * HEAD_DIM)
            seg = cum[:, h:h + 1] - cum_t[h:h + 1, :]
            decay = jnp.exp(jnp.where(causal, seg, -jnp.inf))
            m = (cb * decay).astype(BF16)
            h_prev = h_scr[h]
            y = _dot(m, xdt[:, sl].astype(BF16))
            y = y + _dot_nt(c_bf, h_prev.astype(BF16)) * e_cum[:, h:h + 1]
            bw = (b_g * w_s[:, h:h + 1]).astype(BF16)
            h_scr[h] = h_prev * e_last[:, h:h + 1] + _dot(xdt_t[sl, :].astype(BF16), bw)
            y_scr[:, sl] = y

    y = (y_scr[...] + drow_ref[...] * xs) * _silu(z_ref[...])
    y_ref[...] = y * lax.rsqrt(jnp.mean(y * y, -1, keepdims=True) + EPS) * g_ref[...]

    @pl.when(i == nt - 1)
    def _():
        hout_ref[...] = h_scr[...]


def _ssd(xbc, dt, z, h0, wl, *, t_valid=None):
    nb, t, _ = xbc.shape
    nt = t // SSD_CHUNK

    def tile(c):
        return pl.BlockSpec((None, SSD_CHUNK, c), lambda b, i: (b, i, 0))

    def const(c):
        return pl.BlockSpec((1, c), lambda b, i: (0, 0))

    hspec = pl.BlockSpec((None, SSD_HEADS, HEAD_DIM, SSD_STATE), lambda b, i: (b, 0, 0, 0))
    return pl.pallas_call(
        functools.partial(_ssd_kernel, nt=nt, t_valid=t_valid),
        grid=(nb, nt),
        in_specs=[tile(SSD_CONV_DIM), tile(LANES), tile(SSD_INNER), hspec,
                  const(LANES), const(SSD_INNER), const(SSD_INNER)],
        out_specs=[tile(SSD_INNER), hspec],
        out_shape=[jax.ShapeDtypeStruct((nb, t, SSD_INNER), F32),
                   jax.ShapeDtypeStruct((nb, SSD_HEADS, HEAD_DIM, SSD_STATE), F32)],
        scratch_shapes=[pltpu.VMEM((SSD_HEADS, HEAD_DIM, SSD_STATE), F32),
                        pltpu.VMEM((SSD_CHUNK, SSD_INNER), F32)],
        compiler_params=_params(2),
        name="ssd",
    )(xbc, dt, z, h0, wl["a_log"], wl["ssd_d"], wl["ssd_norm_g"])


def _kth_largest(count_ge, shape, k_top):
    def body(it, prefix):
        cand = prefix ^ lax.shift_left(jnp.int32(1), 31 - it)
        return jnp.where(count_ge(cand) >= k_top, cand, prefix)
    return lax.fori_loop(0, 32, body, jnp.full(shape, INT32_MIN, jnp.int32))


def _tri(lower):
    r = lax.broadcasted_iota(jnp.int32, (LANES, LANES), 0)
    c = lax.broadcasted_iota(jnp.int32, (LANES, LANES), 1)
    return jnp.where((c <= r) if lower else (r <= c), 1.0, 0.0).astype(BF16)


def _pattn_kernel(qit_ref, qt_ref, misct_ref, kidx_ref, k_ref, vt_ref, o_ref, keys_scr, hi_scr, lo_scr,
                  *, tq, ck, k_top):
    i = pl.program_id(1)
    nch = (i * tq + tq + ck - 1) // ck
    sub = ck // LANES

    qit = qit_ref[...] * IDX_DIM ** -0.5
    qi_all = jnp.concatenate([qit[h * IDX_DIM:(h + 1) * IDX_DIM, :] for h in range(IDX_HEADS)], axis=1)
    w_rows = misct_ref[WI_LANE:WI_LANE + SUBLANES, :] * IDX_HEADS ** -0.5
    q_pos = i * tq + lax.broadcasted_iota(jnp.int32, (ck, tq), 1)
    k_row = lax.broadcasted_iota(jnp.int32, (ck, tq), 0)

    def score_chunk(c, carry):
        base = pl.multiple_of(c * ck, ck)
        st = _dot(kidx_ref[pl.ds(base, ck), :], qi_all)
        sc = jnp.zeros((ck, tq), F32)
        for h in range(IDX_HEADS):
            sc = sc + jnp.maximum(st[:, h * tq:(h + 1) * tq], 0.0) * w_rows[h:h + 1, :]
        key = jnp.where(k_row + base <= q_pos, _score_key(sc), KEY_NEG_INF)
        keys_scr[c] = key
        hi_scr[c] = (key >> HALF_BITS).astype(jnp.int16)
        return carry

    lax.fori_loop(0, nch, score_chunk, 0)

    def count(pred):
        def body(c, acc):
            for j in range(sub):
                hit = jnp.where(pred(keys_scr[c, j * LANES:(j + 1) * LANES, :]), 1, 0)
                acc = acc + hit.reshape(LANES // SUBLANES, SUBLANES, tq).sum(axis=0)
            return acc
        acc = lax.fori_loop(0, nch, body, jnp.zeros((SUBLANES, tq), jnp.int32))
        return jnp.sum(acc, axis=0, keepdims=True)

    def half_tile(x):
        return jnp.broadcast_to(x, (PACKED_ROWS, tq)).astype(jnp.int16)

    def count16(scr, pred):
        def body(c, accs):
            accs = list(accs)
            for j in range(ck // PACKED_ROWS):
                hit = jnp.where(pred(scr[c, j * PACKED_ROWS:(j + 1) * PACKED_ROWS, :]), jnp.int16(1), jnp.int16(0))
                accs[j % len(accs)] = accs[j % len(accs)] + hit
            return tuple(accs)
        accs = lax.fori_loop(0, nch, body, tuple(jnp.zeros((PACKED_ROWS, tq), jnp.int16) for _ in range(4)))
        total = (accs[0] + accs[1]) + (accs[2] + accs[3])
        return jnp.sum(total.astype(jnp.int32), axis=0, keepdims=True)

    def kth_half(scr, k_need):
        def body(it, u):
            cand = u | lax.shift_left(jnp.int32(1), HALF_BITS - 1 - it)
            tile = half_tile(cand + HALF_MIN)
            return jnp.where(count16(scr, lambda t: t >= tile) >= k_need, cand, u)
        return lax.fori_loop(0, HALF_BITS, body, jnp.zeros((1, tq), jnp.int32))

    hi = kth_half(hi_scr, k_top) + HALF_MIN
    hi_tile = half_tile(hi)
    k_low = k_top - count16(hi_scr, lambda t: t > hi_tile)

    def low_chunk(c, carry):
        key = keys_scr[c]
        low = (key & HALF_MASK) + HALF_MIN
        lo_scr[c] = jnp.where((key >> HALF_BITS) == hi, low, HALF_MIN).astype(jnp.int16)
        return carry

    lax.fori_loop(0, nch, low_chunk, 0)
    thr = hi * (HALF_MASK + 1) + kth_half(lo_scr, k_low)
    need = k_top - count(lambda kk: kk > thr)
    n_tie = count(lambda kk: kk == thr)
    surplus = jnp.max(jnp.where((n_tie > need) & (thr > KEY_NEG_INF), 1, 0))

    @pl.when(surplus > 0)
    def _():
        ltri = _tri(lower=True)
        need_f = need.astype(F32)

        def tie_chunk(c, carry):
            for j in range(sub):
                kk = keys_scr[c, j * LANES:(j + 1) * LANES, :]
                eq = kk == thr
                eqf = jnp.where(eq, 1.0, 0.0)
                pc = _dot(ltri, eqf.astype(BF16)) + carry
                keys_scr[c, j * LANES:(j + 1) * LANES, :] = jnp.where(eq & (pc > need_f), KEY_NEG_INF, kk)
                carry = carry + jnp.sum(eqf, axis=0, keepdims=True)
            return carry

        lax.fori_loop(0, nch, tie_chunk, jnp.zeros((1, tq), F32))

    thr_sel = jnp.maximum(thr, KEY_NEG_INF + 1)

    qt = qt_ref[...] * HEAD_DIM ** -0.5
    row_head = lax.broadcasted_iota(jnp.int32, (ATT_WIDTH, tq), 0) // HEAD_DIM
    q_heads = [jnp.where(row_head == h, qt, 0.0).astype(BF16) for h in range(ATT_HEADS)]

    def att_chunk(c, st):
        ms, ls, accs = st
        k_c = k_ref[pl.ds(pl.multiple_of(c * ck, ck), ck), :]
        bias = jnp.where(keys_scr[c] >= thr_sel, 0.0, NEG_BIG)
        vt = vt_ref[c]
        new_m, new_l, new_acc = [], [], []
        for h in range(ATT_HEADS):
            s = _dot(k_c, q_heads[h]) + bias
            m_new = jnp.maximum(ms[h], jnp.max(s, axis=0, keepdims=True))
            alpha = jnp.exp(ms[h] - m_new)
            p = jnp.exp(s - m_new)
            new_m.append(m_new)
            new_l.append(ls[h] * alpha + jnp.sum(p, axis=0, keepdims=True))
            new_acc.append(accs[h] * alpha + _dot(vt[h * HEAD_DIM:(h + 1) * HEAD_DIM, :], p.astype(BF16)))
        return tuple(new_m), tuple(new_l), tuple(new_acc)

    init = (tuple(jnp.full((1, tq), NEG_BIG, F32) for _ in range(ATT_HEADS)),
            tuple(jnp.zeros((1, tq), F32) for _ in range(ATT_HEADS)),
            tuple(jnp.zeros((HEAD_DIM, tq), F32) for _ in range(ATT_HEADS)))
    _, ls, accs = lax.fori_loop(0, nch, att_chunk, init)
    out_t = jnp.concatenate([accs[h] / ls[h] for h in range(ATT_HEADS)], axis=0)
    o_ref[...] = out_t.T


def _prompt_attend(qit, qt, misct, kidx_bf, k_bf, vt, *, tq):
    nb, nchunks, _, ck = vt.shape
    t = nchunks * ck
    k_top = min(TOPK_MAX, t // 4)

    def tile_t(c):
        return pl.BlockSpec((None, c, tq), lambda b, i: (b, 0, i))

    def full(c):
        return pl.BlockSpec((None, t, c), lambda b, i: (b, 0, 0))

    return pl.pallas_call(
        functools.partial(_pattn_kernel, tq=tq, ck=ck, k_top=k_top),
        grid=(nb, t // tq),
        in_specs=[tile_t(IDX_HEADS * IDX_DIM), tile_t(ATT_WIDTH), tile_t(LANES),
                  full(IDX_DIM), full(ATT_WIDTH),
                  pl.BlockSpec((None, nchunks, ATT_WIDTH, ck), lambda b, i: (b, 0, 0, 0))],
        out_specs=pl.BlockSpec((None, tq, ATT_WIDTH), lambda b, i: (b, i, 0)),
        out_shape=jax.ShapeDtypeStruct((nb, t, ATT_WIDTH), F32),
        scratch_shapes=[pltpu.VMEM((nchunks, ck, tq), jnp.int32), pltpu.VMEM((nchunks, ck, tq), jnp.int16),
                        pltpu.VMEM((nchunks, ck, tq), jnp.int16)],
        compiler_params=_params(2),
        name="prompt_attend",
    )(qit, qt, misct, kidx_bf, k_bf, vt)


TOK_PAD = 8


def _sscore_kernel(pt_ref, qs_ref, misc_ref, knew_ref, *rest, pps, n_pages, t_new):
    page_refs = rest[:pps]
    keys_ref = rest[pps]
    j = pl.program_id(1)
    nsteps = n_pages // pps
    qs = qs_ref[...]
    wcol = misc_ref[:, WI_LANE:WI_LANE + IDX_HEADS] * IDX_HEADS ** -0.5

    def score_keys(kt):
        s = _dot(qs, kt.astype(BF16)) * IDX_DIM ** -0.5
        sc = jnp.zeros((TOK_PAD, kt.shape[1]), F32)
        for h in range(IDX_HEADS):
            sc = sc + jnp.maximum(s[h * TOK_PAD:(h + 1) * TOK_PAD], 0.0) * wcol[:, h:h + 1]
        return _score_key(sc)

    keys = score_keys(jnp.concatenate([r[...] for r in page_refs], axis=1))
    for p in range(pps):
        keys_ref[j * pps + p] = keys[:, p * PAGE_SIZE:(p + 1) * PAGE_SIZE]

    @pl.when(j == nsteps - 1)
    def _():
        row = lax.broadcasted_iota(jnp.int32, (TOK_PAD, LANES), 0)
        col = lax.broadcasted_iota(jnp.int32, (TOK_PAD, LANES), 1)
        keys_ref[n_pages] = jnp.where((col <= row) & (col < t_new), score_keys(knew_ref[...]), KEY_NEG_INF)


def _sample_scores(page_table, qs, misc_bm, kidx_new_t, cache_kidx_t, layer, *, pps, t_new):
    nb, n_pages = page_table.shape

    def per_b(shape):
        return pl.BlockSpec((None,) + shape, lambda b, j, pt: (b,) + (0,) * len(shape))

    def page(p):
        return pl.BlockSpec((None, None, IDX_DIM, PAGE_SIZE),
                            lambda b, j, pt: (pt[b, j * pps + p], layer, 0, 0))

    grid_spec = pltpu.PrefetchScalarGridSpec(
        num_scalar_prefetch=1,
        grid=(nb, n_pages // pps),
        in_specs=[per_b((IDX_HEADS * TOK_PAD, IDX_DIM)), per_b((TOK_PAD, LANES)), per_b((IDX_DIM, PAGE_SIZE))]
                 + [page(p) for p in range(pps)],
        out_specs=per_b((n_pages + 1, TOK_PAD, LANES)),
    )
    return pl.pallas_call(
        functools.partial(_sscore_kernel, pps=pps, n_pages=n_pages, t_new=t_new),
        grid_spec=grid_spec,
        out_shape=jax.ShapeDtypeStruct((nb, n_pages + 1, TOK_PAD, LANES), jnp.int32),
        compiler_params=_params(2),
        name="sample_scores",
    )(page_table, qs, misc_bm, kidx_new_t, *([cache_kidx_t] * pps))


def _ssel_kernel(keys_ref, bias_ref, *, nch, grp, k_top):
    rows = keys_ref.shape[1]

    def count(pred):
        def body(g, acc):
            for u in range(grp):
                acc = acc + jnp.where(pred(keys_ref[g * grp + u]), 1, 0)
            return acc
        acc = lax.fori_loop(0, nch // grp, body, jnp.zeros((rows, LANES), jnp.int32))
        return jnp.sum(acc, axis=1, keepdims=True)

    thr = _kth_largest(lambda cand: count(lambda kk: kk >= cand), (rows, 1), k_top)
    need = (k_top - count(lambda kk: kk > thr)).astype(F32)
    utri = _tri(lower=False)

    def tie_chunk(c, carry):
        kk = keys_ref[c]
        eq = kk == thr
        eqf = jnp.where(eq, 1.0, 0.0)
        pc = _dot(eqf.astype(BF16), utri) + carry
        sel = (kk > thr) | (eq & (pc <= need) & (thr > KEY_NEG_INF))
        bias_ref[c] = jnp.where(sel, 0.0, NEG_BIG)
        return carry + jnp.sum(eqf, axis=1, keepdims=True)

    lax.fori_loop(0, nch, tie_chunk, jnp.zeros((rows, 1), F32))


def _sample_select(keys, *, k_top):
    nch, rows, _ = keys.shape
    grp = 5 if nch % 5 == 0 else 1
    return pl.pallas_call(
        functools.partial(_ssel_kernel, nch=nch, grp=grp, k_top=k_top),
        grid=(1,),
        in_specs=[pl.BlockSpec(keys.shape, lambda i: (0, 0, 0))],
        out_specs=pl.BlockSpec(keys.shape, lambda i: (0, 0, 0)),
        out_shape=jax.ShapeDtypeStruct(keys.shape, F32),
        compiler_params=_params(1),
        name="sample_select",
    )(keys)


def _sattn_kernel(pt_ref, qbd_ref, bias_ref, knew_ref, vnew_ref, *rest, pps, n_pages):
    k_refs = rest[:pps]
    v_refs = rest[pps:2 * pps]
    o_ref = rest[2 * pps]
    m_scr, l_scr, acc_scr = rest[2 * pps + 1:]
    j = pl.program_id(1)
    nsteps = n_pages // pps
    rows = ATT_HEADS * TOK_PAD

    @pl.when(j == 0)
    def _():
        m_scr[...] = jnp.full((rows, 1), NEG_BIG, F32)
        l_scr[...] = jnp.zeros((rows, 1), F32)
        acc_scr[...] = jnp.zeros((rows, ATT_WIDTH), F32)

    qbd = qbd_ref[...]

    def page_t(ref):
        return ref[...].reshape(ATT_WIDTH, PAGE_SIZE).astype(BF16)

    def update(kt, vt, bias):
        s = _dot(qbd, kt) * HEAD_DIM ** -0.5 + jnp.concatenate([bias] * ATT_HEADS, axis=0)
        m = m_scr[...]
        m_new = jnp.maximum(m, jnp.max(s, axis=1, keepdims=True))
        alpha = jnp.exp(m - m_new)
        p = jnp.exp(s - m_new)
        l_scr[...] = l_scr[...] * alpha + jnp.sum(p, axis=1, keepdims=True)
        acc_scr[...] = acc_scr[...] * alpha + _dot_nt(p.astype(BF16), vt)
        m_scr[...] = m_new

    update(jnp.concatenate([page_t(r) for r in k_refs], axis=1), jnp.concatenate([page_t(r) for r in v_refs], axis=1),
           jnp.concatenate([bias_ref[j * pps + p] for p in range(pps)], axis=1))

    @pl.when(j == nsteps - 1)
    def _():
        update(knew_ref[...], vnew_ref[...], bias_ref[n_pages])
        full = acc_scr[...] / l_scr[...]
        lane_head = lax.broadcasted_iota(jnp.int32, (TOK_PAD, ATT_WIDTH), 1) // HEAD_DIM
        out = jnp.zeros((TOK_PAD, ATT_WIDTH), F32)
        for h in range(ATT_HEADS):
            out = out + jnp.where(lane_head == h, full[h * TOK_PAD:(h + 1) * TOK_PAD], 0.0)
        o_ref[...] = out


def _sample_attend(page_table, qbd, bias, kt_new, vt_new, cache_kt, cache_vt, layer, *, pps):
    nb, n_pages = page_table.shape
    rows = ATT_HEADS * TOK_PAD

    def per_b(shape):
        return pl.BlockSpec((None,) + shape, lambda b, j, pt: (b,) + (0,) * len(shape))

    def page(p):
        return pl.BlockSpec((None, None, ATT_HEADS, HEAD_DIM, PAGE_SIZE),
                            lambda b, j, pt: (pt[b, j * pps + p], layer, 0, 0, 0))

    grid_spec = pltpu.PrefetchScalarGridSpec(
        num_scalar_prefetch=1,
        grid=(nb, n_pages // pps),
        in_specs=[per_b((rows, ATT_WIDTH)), per_b((n_pages + 1, TOK_PAD, LANES)),
                  per_b((ATT_WIDTH, PAGE_SIZE)), per_b((ATT_WIDTH, PAGE_SIZE))]
                 + [page(p) for p in range(pps)] * 2,
        out_specs=per_b((TOK_PAD, ATT_WIDTH)),
        scratch_shapes=[pltpu.VMEM((rows, 1), F32), pltpu.VMEM((rows, 1), F32), pltpu.VMEM((rows, ATT_WIDTH), F32)],
    )
    return pl.pallas_call(
        functools.partial(_sattn_kernel, pps=pps, n_pages=n_pages),
        grid_spec=grid_spec,
        out_shape=jax.ShapeDtypeStruct((nb, TOK_PAD, ATT_WIDTH), F32),
        compiler_params=_params(2),
        name="sample_attend",
    )(page_table, qbd, bias, kt_new, vt_new, *([cache_kt] * pps), *([cache_vt] * pps))


def _post_kernel(x_ref, ys_ref, ya_ref, yc_ref, wo_ref, g1_ref, b1_ref, wu_ref, fcw_ref, fcb_ref, wd_ref,
                 g2_ref, b2_ref, pffn_ref, o_ref, nffn_ref, halo, ubuf, acc_ref, *, tt, rs, nt, cw):
    i = pl.program_id(1)
    h_rows = (FFN_CONV_W - 1) * rs
    hf = _round_up(h_rows, SUBLANES)

    @pl.when(i == 0)
    def _():
        halo[hf - h_rows:hf, :] = pffn_ref[...]

    mix = _dot(ys_ref[...].astype(BF16), wo_ref[0:SSD_INNER, :])
    mix = mix + _dot(ya_ref[...].astype(BF16), wo_ref[SSD_INNER:SSD_INNER + ATT_WIDTH, :])
    mix = mix + _dot(yc_ref[...].astype(BF16), wo_ref[SSD_INNER + ATT_WIDTH:, :])
    x1 = _layer_norm(ALPHA * x_ref[...] + mix, g1_ref[...], b1_ref[...])
    x1b = x1.astype(BF16)

    def up_proj(src, dst):
        ubuf[hf - h_rows:hf, dst:dst + cw] = halo[hf - h_rows:hf, src:src + cw]
        ubuf[hf:hf + tt, dst:dst + cw] = _dot(x1b, wu_ref[:, src:src + cw])
        halo[hf - h_rows:hf, src:src + cw] = ubuf[hf + tt - h_rows:hf + tt, dst:dst + cw]

    def conv(src, dst):
        acc = fcb_ref[:, src:src + cw] + fcw_ref[0:1, src:src + cw] * ubuf[hf - h_rows:hf - h_rows + tt, dst:dst + cw]
        for jj in range(1, FFN_CONV_W):
            off = hf - (FFN_CONV_W - 1 - jj) * rs
            acc = acc + fcw_ref[jj:jj + 1, src:src + cw] * ubuf[off:off + tt, dst:dst + cw]
        return acc

    for c in range(D_FF // cw):
        cv = c * cw
        cg = D_FF + c * cw
        up_proj(cv, 0)
        up_proj(cg, cw)
        f = (_silu(conv(cg, cw)) * conv(cv, 0)).astype(BF16)
        part = _dot(f, wd_ref[cv:cv + cw, :])
        if c == 0:
            acc_ref[...] = part
        else:
            acc_ref[...] += part

    o_ref[...] = _layer_norm(ALPHA * x1 + acc_ref[...], g2_ref[...], b2_ref[...])

    @pl.when(i == nt - 1)
    def _():
        nffn_ref[...] = halo[hf - h_rows:hf, :]


def _post(x, ys, ya, yc, wl, prev_ffn, *, tt, rs):
    nb, t, _ = x.shape
    nt = t // tt
    hf = _round_up((FFN_CONV_W - 1) * rs, SUBLANES)

    def tile(c):
        return pl.BlockSpec((None, tt, c), lambda b, i: (b, i, 0))

    def const(shape):
        return pl.BlockSpec(shape, lambda b, i: (0,) * len(shape), pipeline_mode=pl.Buffered(1))

    sspec = pl.BlockSpec((None, prev_ffn.shape[1], 2 * D_FF), lambda b, i: (b, 0, 0))
    cw = D_FF // 2
    return pl.pallas_call(
        functools.partial(_post_kernel, tt=tt, rs=rs, nt=nt, cw=cw),
        grid=(nb, nt),
        in_specs=[tile(D_MODEL), tile(SSD_INNER), tile(ATT_WIDTH), tile(CC_WIDTH),
                  const((D_MODEL, D_MODEL)), const((1, D_MODEL)), const((1, D_MODEL)),
                  const((D_MODEL, 2 * D_FF)), const((FFN_CONV_W, 2 * D_FF)), const((1, 2 * D_FF)),
                  const((D_FF, D_MODEL)), const((1, D_MODEL)), const((1, D_MODEL)), sspec],
        out_specs=[tile(D_MODEL), sspec],
        out_shape=[jax.ShapeDtypeStruct((nb, t, D_MODEL), F32), jax.ShapeDtypeStruct(prev_ffn.shape, F32)],
        scratch_shapes=[pltpu.VMEM((hf, 2 * D_FF), F32), pltpu.VMEM((hf + tt, 2 * cw), F32),
                        pltpu.VMEM((tt, D_MODEL), F32)],
        compiler_params=_params(2),
        name="post",
    )(x, ys, ya, yc, wl["w_out"], wl["ln1_g"], wl["ln1_b"], wl["ffn_w_up"], wl["ffn_conv_w"], wl["ffn_conv_b"],
      wl["ffn_w_down"], wl["ln2_g"], wl["ln2_b"], prev_ffn)


def _pack_w_in(w_in):
    z, xbc, dt, q, k, v, qi, ki, wi, glu = jnp.split(w_in, [int(c) for c in np.cumsum(IN_SIZES)[:-1]], axis=-1)
    d = w_in.shape[0]
    misc = jnp.concatenate([ki, wi, jnp.zeros((d, D_MODEL, LANES - IDX_DIM - IDX_HEADS), w_in.dtype)], axis=-1)
    dtp = jnp.concatenate([dt, jnp.zeros((d, D_MODEL, LANES - SSD_HEADS), w_in.dtype)], axis=-1)
    return jnp.concatenate([z, xbc, q, k, v, qi, glu, misc, dtp], axis=-1).astype(BF16)


def _layer_weights(p, l):
    pad_l = LANES - SSD_HEADS
    return {
        "w_in": p["w_in_packed"][l],
        "ssd_conv_w": p["ssd_conv_w"][l], "ssd_conv_b": p["ssd_conv_b"][l][None],
        "cc_conv_w": p["cc_conv_w"][l], "cc_conv_b": p["cc_conv_b"][l][None],
        "cc_ln_g": p["cc_ln_g"][l][None], "cc_ln_b": p["cc_ln_b"][l][None],
        "dt_bias": jnp.pad(p["ssd_dt_bias"][l], (0, pad_l))[None],
        "a_log": jnp.pad(p["ssd_a_log"][l], (0, pad_l))[None],
        "ssd_d": jnp.repeat(p["ssd_d"][l], HEAD_DIM)[None],
        "ssd_norm_g": p["ssd_norm_g"][l][None],
        "w_out": p["w_out_bf"][l], "ln1_g": p["ln1_g"][l][None], "ln1_b": p["ln1_b"][l][None],
        "ffn_w_up": p["ffn_w_up_bf"][l], "ffn_conv_w": p["ffn_conv_w"][l], "ffn_conv_b": p["ffn_conv_b"][l][None],
        "ffn_w_down": p["ffn_w_down_bf"][l], "ln2_g": p["ln2_g"][l][None], "ln2_b": p["ln2_b"][l][None],
    }


def _rope_tables(pos):
    half = HEAD_DIM // 2
    inv = ROPE_THETA ** (-jnp.arange(half, dtype=F32) / half)
    ang = pos.astype(F32)[:, None] * inv[None, :]
    cos = jnp.cos(ang)
    sin = jnp.sin(ang)
    cos_h = jnp.concatenate([cos, cos], axis=-1)
    sin_h = jnp.concatenate([-sin, sin], axis=-1)
    n = pos.shape[0]
    pad1 = jnp.ones((n, LANES - IDX_DIM), F32)
    pad0 = jnp.zeros((n, LANES - IDX_DIM), F32)
    return {"cosq": jnp.tile(cos_h, (1, ATT_HEADS)), "sinq": jnp.tile(sin_h, (1, ATT_HEADS)),
            "cosm": jnp.concatenate([cos_h, pad1], axis=-1), "sinm": jnp.concatenate([sin_h, pad0], axis=-1)}


def _prompt_layer(h, wl, tabs, *, tt_pre, tt_post, tq):
    nb, t, _ = h.shape
    zeros = lambda rows, c: jnp.zeros((nb, rows, c), F32)
    (z, xbc, _, k, v, _, _, kidx, dt, ycc, qt, qit, misct, k_bf, kidx_bf, vt, nssd, ncc) = _premix(
        h, wl, tabs, zeros(SSD_CONV_W - 1, SSD_CONV_DIM), zeros(CC_CONV_W - 1, CC_WIDTH), tt=tt_pre, rs=1)
    yssd, ssm = _ssd(xbc, dt, z, jnp.zeros((nb, SSD_HEADS, HEAD_DIM, SSD_STATE), F32), wl)
    yatt = _prompt_attend(qit, qt, misct, kidx_bf, k_bf, vt, tq=tq)
    h, nffn = _post(h, yssd, yatt, ycc, wl, zeros(FFN_CONV_W - 1, 2 * D_FF), tt=tt_post, rs=1)
    states = (k.reshape(nb, t, ATT_HEADS, HEAD_DIM), v.reshape(nb, t, ATT_HEADS, HEAD_DIM), kidx, ssm,
              nssd, ncc, nffn)
    return h, states


def _to_time_major(state):
    b, r, c = state.shape
    return state.transpose(1, 0, 2).reshape(1, r * b, c)


def _to_batch_major(rows, nb):
    _, n, c = rows.shape
    return rows.reshape(n // nb, nb, c).transpose(1, 0, 2)


def _pad_rows(a, rows):
    return jnp.pad(a, ((0, 0), (0, rows - a.shape[1]), (0, 0)))


def _sample_layer(g, wl, tabs, l, cache_k, cache_v, cache_kidx, page_table, ssm_prev, ssd_prev, cc_prev,
                  ffn_prev, *, nb, t_new, pps):
    (z, xbc, q, k, v, qi, misc, kidx, dt, ycc, _, _, _, _, _, _, nssd, ncc) = _premix(
        g, wl, tabs, _to_time_major(ssd_prev), _to_time_major(cc_prev), tt=nb * t_new, rs=nb)
    bm = lambda a: _to_batch_major(a, nb)
    yssd, ssm = _ssd(_pad_rows(bm(xbc), SSD_CHUNK), _pad_rows(bm(dt), SSD_CHUNK), _pad_rows(bm(z), SSD_CHUNK),
                     ssm_prev, wl, t_valid=t_new)
    yssd = _to_time_major(yssd[:, :t_new])
    qi_bm = _pad_rows(bm(qi), TOK_PAD).reshape(nb, TOK_PAD, IDX_HEADS, IDX_DIM)
    qs = qi_bm.transpose(0, 2, 1, 3).reshape(nb, IDX_HEADS * TOK_PAD, IDX_DIM)
    def new_t(a):
        return _pad_rows(a, PAGE_SIZE).transpose(0, 2, 1)

    keys = _sample_scores(page_table, qs, _pad_rows(bm(misc), TOK_PAD), new_t(bm(kidx)),
                          cache_kidx, l, pps=pps, t_new=t_new)
    n_chunks = keys.shape[1]
    n_keys = (n_chunks - 1) * PAGE_SIZE + t_new
    key_rows = keys[:, :, :t_new].transpose(1, 0, 2, 3).reshape(n_chunks, nb * t_new, LANES)
    bias_rows = _sample_select(key_rows, k_top=min(TOPK_MAX, n_keys // 4))
    bias = bias_rows.reshape(n_chunks, nb, t_new, LANES).transpose(1, 0, 2, 3)
    bias = jnp.pad(bias, ((0, 0), (0, 0), (0, TOK_PAD - t_new), (0, 0)))
    q_bm = _pad_rows(bm(q), TOK_PAD).reshape(nb, TOK_PAD, ATT_HEADS, HEAD_DIM)
    eye = jnp.eye(ATT_HEADS, dtype=q_bm.dtype)
    qbd = jnp.einsum("bthd,hg->bhtgd", q_bm, eye).reshape(nb, ATT_HEADS * TOK_PAD, ATT_WIDTH)
    k_bm = bm(k)
    v_bm = bm(v)
    yatt = _sample_attend(page_table, qbd, bias, new_t(k_bm).astype(BF16), new_t(v_bm).astype(BF16),
                          cache_k, cache_v, l, pps=pps)
    yatt = _to_time_major(yatt[:, :t_new])
    g, nffn = _post(g, yssd, yatt, ycc, wl, _to_time_major(ffn_prev), tt=nb * t_new, rs=nb)
    states = (k_bm.reshape(nb, t_new, ATT_HEADS, HEAD_DIM), v_bm.reshape(nb, t_new, ATT_HEADS, HEAD_DIM),
              bm(kidx), ssm, bm(nssd), bm(ncc), bm(nffn))
    return g, states


def kernel(x_prompt, x_sample, cache_k, cache_v, cache_kidx, page_table, state_ssm, state_ssd_conv, state_cc_conv, state_ffn_conv, ln0_g, ln0_b, w_in, ssd_conv_w, ssd_conv_b, ssd_dt_bias, ssd_a_log, ssd_d, ssd_norm_g, cc_conv_w, cc_conv_b, cc_ln_g, cc_ln_b, w_out, ln1_g, ln1_b, ffn_w_up, ffn_conv_w, ffn_conv_b, ffn_w_down, ln2_g, ln2_b):
    depth = w_in.shape[0]
    p = {"w_in_packed": _pack_w_in(w_in), "ssd_conv_w": ssd_conv_w, "ssd_conv_b": ssd_conv_b,
         "ssd_dt_bias": ssd_dt_bias, "ssd_a_log": ssd_a_log, "ssd_d": ssd_d, "ssd_norm_g": ssd_norm_g,
         "cc_conv_w": cc_conv_w, "cc_conv_b": cc_conv_b, "cc_ln_g": cc_ln_g, "cc_ln_b": cc_ln_b,
         "w_out_bf": w_out.astype(BF16), "ln1_g": ln1_g, "ln1_b": ln1_b,
         "ffn_w_up_bf": ffn_w_up.astype(BF16), "ffn_conv_w": ffn_conv_w, "ffn_conv_b": ffn_conv_b,
         "ffn_w_down_bf": ffn_w_down.astype(BF16), "ln2_g": ln2_g, "ln2_b": ln2_b}
    layers = [_layer_weights(p, l) for l in range(depth)]

    bp, sp, _ = x_prompt.shape
    tt_pre = min(512, sp)
    h = _ln_rows(x_prompt.reshape(bp * sp, D_MODEL), ln0_g, ln0_b, tt_pre).reshape(bp, sp, D_MODEL)
    tabs_p = _rope_tables(jnp.arange(sp))
    p_states = []
    for l in range(depth):
        h, st = _prompt_layer(h, layers[l], tabs_p, tt_pre=tt_pre, tt_post=min(512, sp), tq=min(512, sp))
        p_states.append(st)

    nb, ts, _ = x_sample.shape
    n_pages = page_table.shape[1]
    pps = math.gcd(32, n_pages)
    cache_k = cache_k.transpose(0, 1, 3, 4, 2)
    cache_v = cache_v.transpose(0, 1, 3, 4, 2)
    cache_kidx = cache_kidx.transpose(0, 1, 3, 2)
    g = _ln_rows(x_sample.transpose(1, 0, 2).reshape(ts * nb, D_MODEL), ln0_g, ln0_b, ts * nb)
    g = g.reshape(1, ts * nb, D_MODEL)
    tabs_s = _rope_tables(PAST_LEN + jnp.arange(ts * nb) // nb)
    s_states = []
    for l in range(depth):
        g, st = _sample_layer(g, layers[l], tabs_s, l, cache_k, cache_v, cache_kidx, page_table,
                              state_ssm[:, l], state_ssd_conv[:, l], state_cc_conv[:, l], state_ffn_conv[:, l],
                              nb=nb, t_new=ts, pps=pps)
        s_states.append(st)
    y_sample = _to_batch_major(g, nb)

    p_out = [jnp.stack(a, axis=1) for a in zip(*p_states)]
    s_out = [jnp.stack(a, axis=1) for a in zip(*s_states)]
    return (h, y_sample, *p_out, *s_out)
```

```python
import functools
import math

import jax
import jax.numpy as jnp
import numpy as np
from jax import lax
from jax.experimental import pallas as pl
from jax.experimental.pallas import tpu as pltpu

D_MODEL = 1024
DEPTH = 4
PAST_LEN = 8192
PAGE_SIZE = 128
HEAD_DIM = 64
SSD_HEADS = 8
SSD_INNER = SSD_HEADS * HEAD_DIM
SSD_GROUPS = 2
SSD_STATE = 64
SSD_CONV_W = 4
SSD_CONV_DIM = SSD_INNER + 2 * SSD_GROUPS * SSD_STATE
SSD_CHUNK = 128
ATT_HEADS = 4
ATT_WIDTH = ATT_HEADS * HEAD_DIM
IDX_HEADS = 4
IDX_DIM = 64
TOPK_MAX = 256
ROPE_THETA = 10000.0
CC_WIDTH = 256
CC_CONV_W = 31
D_FF = 2816
FFN_CONV_W = 3
ALPHA = (2 * DEPTH) ** 0.25
EPS = 1e-5
IN_SIZES = (SSD_INNER, SSD_CONV_DIM, SSD_HEADS, ATT_WIDTH, ATT_WIDTH, ATT_WIDTH,
            IDX_HEADS * IDX_DIM, IDX_DIM, IDX_HEADS, 2 * CC_WIDTH)

LANES = 128
SUBLANES = 8
VMEM_LIMIT = 56 * 1024 * 1024

C_Z = 0
C_XBC = C_Z + SSD_INNER
C_Q = C_XBC + SSD_CONV_DIM
C_K = C_Q + ATT_WIDTH
C_V = C_K + ATT_WIDTH
C_QI = C_V + ATT_WIDTH
C_GLU = C_QI + IDX_HEADS * IDX_DIM
C_MISC = C_GLU + 2 * CC_WIDTH
C_DT = C_MISC + LANES
C_END = C_DT + LANES
WI_LANE = IDX_DIM

NEG_BIG = -1e30
KEY_NEG_INF = np.int32(-2139095041)
INT32_MIN = np.int32(-2 ** 31)
HALF_BITS = 16
HALF_MASK = (1 << HALF_BITS) - 1
HALF_MIN = -(1 << (HALF_BITS - 1))
PACKED_ROWS = 2 * SUBLANES

F32 = jnp.float32
BF16 = jnp.bfloat16
NT_DIMS = (((1,), (1,)), ((), ()))


def _round_up(n, m):
    return (n + m - 1) // m * m


def _silu(x):
    return x * jax.nn.sigmoid(x)


def _layer_norm(x, g, b):
    mu = jnp.mean(x, -1, keepdims=True)
    var = jnp.mean(jnp.square(x - mu), -1, keepdims=True)
    return (x - mu) * lax.rsqrt(var + EPS) * g + b


def _dot(a, b):
    return jnp.dot(a, b, preferred_element_type=F32)


def _dot_nt(a, b):
    return lax.dot_general(a, b, NT_DIMS, preferred_element_type=F32)


def _params(n_axes):
    return pltpu.CompilerParams(dimension_semantics=("arbitrary",) * n_axes, vmem_limit_bytes=VMEM_LIMIT)


def _rope(x, cos, sin_signed):
    n = x.shape[-1]
    lane = lax.broadcasted_iota(jnp.int32, x.shape, 1)
    first_half = (lane % HEAD_DIM) < (HEAD_DIM // 2)
    partner = jnp.where(first_half, pltpu.roll(x, n - HEAD_DIM // 2, 1), pltpu.roll(x, HEAD_DIM // 2, 1))
    return x * cos + partner * sin_signed


def _score_key(score):
    score = jnp.where(score == 0.0, 0.0, score)
    bits = pltpu.bitcast(score, jnp.int32)
    return jnp.where(bits < 0, bits ^ np.int32(0x7FFFFFFF), bits)


def _split3(x):
    hi = x.astype(BF16)
    r1 = x - hi.astype(F32)
    mid = r1.astype(BF16)
    lo = (r1 - mid.astype(F32)).astype(BF16)
    return hi, mid, lo


def _ln_kernel(x_ref, g_ref, b_ref, o_ref):
    o_ref[...] = _layer_norm(x_ref[...], g_ref[...], b_ref[...])


def _ln_rows(x2d, g, b, rows):
    n, d = x2d.shape
    return pl.pallas_call(
        _ln_kernel,
        grid=(n // rows,),
        in_specs=[pl.BlockSpec((rows, d), lambda i: (i, 0)),
                  pl.BlockSpec((1, d), lambda i: (0, 0)),
                  pl.BlockSpec((1, d), lambda i: (0, 0))],
        out_specs=pl.BlockSpec((rows, d), lambda i: (i, 0)),
        out_shape=jax.ShapeDtypeStruct((n, d), F32),
        compiler_params=_params(1),
        name="ln0",
    )(x2d, g.reshape(1, d), b.reshape(1, d))


def _premix_kernel(x_ref, w_ref, scw_ref, scb_ref, ccw_ref, ccb_ref, lng_ref, lnb_ref, dtb_ref,
                   cosq_ref, sinq_ref, cosm_ref, sinm_ref, pssd_ref, pcc_ref,
                   z_ref, xbc_ref, q_ref, k_ref, v_ref, qi_ref, misc_ref, kidx_ref, dt_ref, ycc_ref,
                   qt_ref, qit_ref, misct_ref, kbf_ref, kidxbf_ref, vt_ref,
                   nssd_ref, ncc_ref, sbuf, cbuf, *, tt, rs, nt, rb):
    i = pl.program_id(1)
    hs_rows = (SSD_CONV_W - 1) * rs
    hc_rows = (CC_CONV_W - 1) * rs
    hs = _round_up(hs_rows, SUBLANES)
    hc = _round_up(hc_rows, SUBLANES)

    @pl.when(i == 0)
    def _():
        sbuf[hs - hs_rows:hs, :] = pssd_ref[...]
        cbuf[hc - hc_rows:hc, :] = pcc_ref[...]

    xb = x_ref[...].astype(BF16)

    def seg(a, b):
        return _dot(xb, w_ref[:, a:b])

    z_ref[...] = seg(C_Z, C_XBC)

    sbuf[hs:hs + tt, :] = seg(C_XBC, C_Q)
    for r0 in range(0, tt, rb):
        acc = scb_ref[...] + scw_ref[0:1, :] * sbuf[hs - hs_rows + r0:hs - hs_rows + r0 + rb, :]
        for j in range(1, SSD_CONV_W):
            off = hs - (SSD_CONV_W - 1 - j) * rs + r0
            acc = acc + scw_ref[j:j + 1, :] * sbuf[off:off + rb, :]
        xbc_ref[r0:r0 + rb, :] = _silu(acc)

    @pl.when(i == nt - 1)
    def _():
        nssd_ref[...] = sbuf[hs + tt - hs_rows:hs + tt, :]

    sbuf[hs - hs_rows:hs, :] = sbuf[hs + tt - hs_rows:hs + tt, :]

    dtr = seg(C_DT, C_END) + dtb_ref[...]
    sp = jnp.maximum(dtr, 0.0) + jnp.log1p(jnp.exp(-jnp.abs(dtr)))
    lane = lax.broadcasted_iota(jnp.int32, sp.shape, 1)
    dt_ref[...] = jnp.where(lane < SSD_HEADS, sp, 0.0)

    cosq = cosq_ref[...]
    sinq = sinq_ref[...]
    q = _rope(seg(C_Q, C_K), cosq, sinq)
    q_ref[...] = q.astype(BF16)
    qt_ref[...] = q.T.astype(BF16)
    k = _rope(seg(C_K, C_V), cosq, sinq)
    k_ref[...] = k
    kbf_ref[...] = k.astype(BF16)
    v = seg(C_V, C_QI)
    v_ref[...] = v
    vt_ref[...] = v.T.astype(BF16)
    qi = _rope(seg(C_QI, C_GLU), cosq, sinq)
    qi_ref[...] = qi.astype(BF16)
    qit_ref[...] = qi.T.astype(BF16)
    misc = _rope(seg(C_MISC, C_DT), cosm_ref[...], sinm_ref[...])
    misc_ref[...] = misc
    misct_ref[...] = misc.T
    kidx_ref[...] = misc[:, :IDX_DIM]
    kidxbf_ref[...] = misc[:, :IDX_DIM].astype(BF16)

    glu = seg(C_GLU, C_MISC)
    cbuf[hc:hc + tt, :] = glu[:, :CC_WIDTH] * jax.nn.sigmoid(glu[:, CC_WIDTH:])
    for r0 in range(0, tt, rb):
        acc = ccb_ref[...] + ccw_ref[0:1, :] * cbuf[hc - hc_rows + r0:hc - hc_rows + r0 + rb, :]
        for j in range(1, CC_CONV_W):
            off = hc - (CC_CONV_W - 1 - j) * rs + r0
            acc = acc + ccw_ref[j:j + 1, :] * cbuf[off:off + rb, :]
        ycc_ref[r0:r0 + rb, :] = _silu(_layer_norm(acc, lng_ref[...], lnb_ref[...]))

    @pl.when(i == nt - 1)
    def _():
        ncc_ref[...] = cbuf[hc + tt - hc_rows:hc + tt, :]

    cbuf[hc - hc_rows:hc, :] = cbuf[hc + tt - hc_rows:hc + tt, :]


def _premix(x, wl, tabs, prev_ssd, prev_cc, *, tt, rs):
    nb, t, _ = x.shape
    nt = t // tt
    rb = min(tt, 64)
    hs = _round_up((SSD_CONV_W - 1) * rs, SUBLANES)
    hc = _round_up((CC_CONV_W - 1) * rs, SUBLANES)

    def tile(c):
        return pl.BlockSpec((None, tt, c), lambda b, i: (b, i, 0))

    def const(shape):
        return pl.BlockSpec(shape, lambda b, i: (0,) * len(shape))

    def tab(c):
        return pl.BlockSpec((tt, c), lambda b, i: (i, 0))

    def state(rows, c):
        return pl.BlockSpec((None, rows, c), lambda b, i: (b, 0, 0))

    widths = (SSD_INNER, SSD_CONV_DIM, ATT_WIDTH, ATT_WIDTH, ATT_WIDTH, IDX_HEADS * IDX_DIM,
              LANES, IDX_DIM, LANES, CC_WIDTH)
    dtypes = (F32, F32, BF16, F32, F32, BF16, F32, F32, F32, F32)
    out_shape = [jax.ShapeDtypeStruct((nb, t, c), d) for c, d in zip(widths, dtypes)]
    out_specs = [tile(c) for c in widths]

    def tile_t(c):
        return pl.BlockSpec((None, c, tt), lambda b, i: (b, 0, i))

    out_shape += [jax.ShapeDtypeStruct((nb, ATT_WIDTH, t), BF16),
                  jax.ShapeDtypeStruct((nb, IDX_HEADS * IDX_DIM, t), BF16),
                  jax.ShapeDtypeStruct((nb, LANES, t), F32),
                  jax.ShapeDtypeStruct((nb, t, ATT_WIDTH), BF16),
                  jax.ShapeDtypeStruct((nb, t, IDX_DIM), BF16),
                  jax.ShapeDtypeStruct((nb, nt, ATT_WIDTH, tt), BF16)]
    out_specs += [tile_t(ATT_WIDTH), tile_t(IDX_HEADS * IDX_DIM), tile_t(LANES), tile(ATT_WIDTH), tile(IDX_DIM),
                  pl.BlockSpec((None, None, ATT_WIDTH, tt), lambda b, i: (b, i, 0, 0))]
    out_shape += [jax.ShapeDtypeStruct(prev_ssd.shape, F32), jax.ShapeDtypeStruct(prev_cc.shape, F32)]
    out_specs += [state(prev_ssd.shape[1], SSD_CONV_DIM), state(prev_cc.shape[1], CC_WIDTH)]
    return pl.pallas_call(
        functools.partial(_premix_kernel, tt=tt, rs=rs, nt=nt, rb=rb),
        grid=(nb, nt),
        in_specs=[tile(D_MODEL), const((D_MODEL, C_END)),
                  const((SSD_CONV_W, SSD_CONV_DIM)), const((1, SSD_CONV_DIM)),
                  const((CC_CONV_W, CC_WIDTH)), const((1, CC_WIDTH)), const((1, CC_WIDTH)), const((1, CC_WIDTH)),
                  const((1, LANES)),
                  tab(ATT_WIDTH), tab(ATT_WIDTH), tab(LANES), tab(LANES),
                  state(prev_ssd.shape[1], SSD_CONV_DIM), state(prev_cc.shape[1], CC_WIDTH)],
        out_specs=out_specs,
        out_shape=out_shape,
        scratch_shapes=[pltpu.VMEM((hs + tt, SSD_CONV_DIM), F32), pltpu.VMEM((hc + tt, CC_WIDTH), F32)],
        compiler_params=_params(2),
        name="premix",
    )(x, wl["w_in"], wl["ssd_conv_w"], wl["ssd_conv_b"], wl["cc_conv_w"], wl["cc_conv_b"],
      wl["cc_ln_g"], wl["cc_ln_b"], wl["dt_bias"], tabs["cosq"], tabs["sinq"], tabs["cosm"], tabs["sinm"],
      prev_ssd, prev_cc)


def _ssd_kernel(xbc_ref, dt_ref, z_ref, h0_ref, alog_ref, drow_ref, g_ref, y_ref, hout_ref, h_scr, y_scr,
                *, nt, t_valid, cps):
    i = pl.program_id(1)
    q = SSD_CHUNK

    @pl.when(i == 0)
    def _():
        h_scr[...] = h0_ref[...]

    row = lax.broadcasted_iota(jnp.int32, (q, q), 0)
    col = lax.broadcasted_iota(jnp.int32, (q, q), 1)
    causal = col <= row
    tril = jnp.where(causal, 1.0, 0.0).astype(BF16)
    for s in range(cps):
        _ssd_chunk(xbc_ref.at[s * q:(s + 1) * q], dt_ref.at[s * q:(s + 1) * q], z_ref.at[s * q:(s + 1) * q],
                   alog_ref, drow_ref, g_ref, y_ref.at[s * q:(s + 1) * q], h_scr, y_scr.at[s],
                   row, causal, tril, t_valid)

    @pl.when(i == nt - 1)
    def _():
        hout_ref[...] = h_scr[...]


def _ssd_chunk(xbc_ref, dt_ref, z_ref, alog_ref, drow_ref, g_ref, y_ref, h_scr, y_scr, row, causal, tril, t_valid):
    q = SSD_CHUNK
    dt = dt_ref[...]
    if t_valid is not None:
        dt = jnp.where(row < t_valid, dt, 0.0)
    lane1 = lax.broadcasted_iota(jnp.int32, (1, LANES), 1)
    a_row = jnp.where(lane1 < SSD_HEADS, -jnp.exp(alog_ref[...]), 0.0)
    da = dt * a_row
    hi, mid, lo = _split3(da)
    cum = _dot(tril, hi) + _dot(tril, mid) + _dot(tril, lo)
    cum_t = cum.T
    last = cum[q - 1:q, :]
    w_s = jnp.exp(last - cum)
    e_cum = jnp.exp(cum)
    e_last = jnp.exp(last)

    erow = lax.broadcasted_iota(jnp.int32, (LANES, SSD_INNER), 0)
    ecol = lax.broadcasted_iota(jnp.int32, (LANES, SSD_INNER), 1)
    expand = jnp.where(ecol // HEAD_DIM == erow, 1.0, 0.0).astype(BF16)
    dhi, dmid, dlo = _split3(dt)
    dt_wide = _dot(dhi, expand) + _dot(dmid, expand) + _dot(dlo, expand)

    xs = xbc_ref[:, :SSD_INNER]
    xdt = xs * dt_wide
    xdt_t = xdt.T
    gn = SSD_GROUPS * SSD_STATE
    rep = SSD_HEADS // SSD_GROUPS
    for g in range(SSD_GROUPS):
        b_g = xbc_ref[:, SSD_INNER + g * SSD_STATE:SSD_INNER + (g + 1) * SSD_STATE]
        c_g = xbc_ref[:, SSD_INNER + gn + g * SSD_STATE:SSD_INNER + gn + (g + 1) * SSD_STATE]
        c_bf = c_g.astype(BF16)
        cb = _dot_nt(c_bf, b_g.astype(BF16))
        for h in range(g * rep, (g + 1) * rep):
            sl = slice(h * HEAD_DIM, (h + 1) * HEAD_DIM)
            seg = cum[:, h:h + 1] - cum_t[h:h + 1, :]
            decay = jnp.exp(jnp.where(causal, seg, -jnp.inf))
            m = (cb * decay).astype(BF16)
            h_prev = h_scr[h]
            y = _dot(m, xdt[:, sl].astype(BF16))
            y = y + _dot_nt(c_bf, h_prev.astype(BF16)) * e_cum[:, h:h + 1]
            bw = (b_g * w_s[:, h:h + 1]).astype(BF16)
            h_scr[h] = h_prev * e_last[:, h:h + 1] + _dot(xdt_t[sl, :].astype(BF16), bw)
            y_scr[:, sl] = y

    y = (y_scr[...] + drow_ref[...] * xs) * _silu(z_ref[...])
    y_ref[...] = y * lax.rsqrt(jnp.mean(y * y, -1, keepdims=True) + EPS) * g_ref[...]


def _ssd(xbc, dt, z, h0, wl, *, t_valid=None):
    nb, t, _ = xbc.shape
    cps = math.gcd(4, t // SSD_CHUNK)
    nt = t // (SSD_CHUNK * cps)

    def tile(c):
        return pl.BlockSpec((None, SSD_CHUNK * cps, c), lambda b, i: (b, i, 0))

    def const(c):
        return pl.BlockSpec((1, c), lambda b, i: (0, 0))

    hspec = pl.BlockSpec((None, SSD_HEADS, HEAD_DIM, SSD_STATE), lambda b, i: (b, 0, 0, 0))
    return pl.pallas_call(
        functools.partial(_ssd_kernel, nt=nt, t_valid=t_valid, cps=cps),
        grid=(nb, nt),
        in_specs=[tile(SSD_CONV_DIM), tile(LANES), tile(SSD_INNER), hspec,
                  const(LANES), const(SSD_INNER), const(SSD_INNER)],
        out_specs=[tile(SSD_INNER), hspec],
        out_shape=[jax.ShapeDtypeStruct((nb, t, SSD_INNER), F32),
                   jax.ShapeDtypeStruct((nb, SSD_HEADS, HEAD_DIM, SSD_STATE), F32)],
        scratch_shapes=[pltpu.VMEM((SSD_HEADS, HEAD_DIM, SSD_STATE), F32),
                        pltpu.VMEM((cps, SSD_CHUNK, SSD_INNER), F32)],
        compiler_params=_params(2),
        name="ssd",
    )(xbc, dt, z, h0, wl["a_log"], wl["ssd_d"], wl["ssd_norm_g"])


def _kth_largest(count_ge, shape, k_top):
    def body(it, prefix):
        cand = prefix ^ lax.shift_left(jnp.int32(1), 31 - it)
        return jnp.where(count_ge(cand) >= k_top, cand, prefix)
    return lax.fori_loop(0, 32, body, jnp.full(shape, INT32_MIN, jnp.int32))


def _tri(lower):
    r = lax.broadcasted_iota(jnp.int32, (LANES, LANES), 0)
    c = lax.broadcasted_iota(jnp.int32, (LANES, LANES), 1)
    return jnp.where((c <= r) if lower else (r <= c), 1.0, 0.0).astype(BF16)


def _pattn_kernel(qit_ref, qt_ref, misct_ref, kidx_ref, k_ref, vt_ref, o_ref, keys_scr, hi_scr, lo_scr,
                  *, tq, ck, k_top):
    i = pl.program_id(1)
    nch = (i * tq + tq + ck - 1) // ck
    sub = ck // LANES

    qit = qit_ref[...] * IDX_DIM ** -0.5
    qi_all = jnp.concatenate([qit[h * IDX_DIM:(h + 1) * IDX_DIM, :] for h in range(IDX_HEADS)], axis=1)
    w_rows = misct_ref[WI_LANE:WI_LANE + SUBLANES, :] * IDX_HEADS ** -0.5
    q_pos = i * tq + lax.broadcasted_iota(jnp.int32, (ck, tq), 1)
    k_row = lax.broadcasted_iota(jnp.int32, (ck, tq), 0)

    def score_chunk(c, carry):
        base = pl.multiple_of(c * ck, ck)
        st = _dot(kidx_ref[pl.ds(base, ck), :], qi_all)
        sc = jnp.zeros((ck, tq), F32)
        for h in range(IDX_HEADS):
            sc = sc + jnp.maximum(st[:, h * tq:(h + 1) * tq], 0.0) * w_rows[h:h + 1, :]
        key = jnp.where(k_row + base <= q_pos, _score_key(sc), KEY_NEG_INF)
        keys_scr[c] = key
        hi_scr[c] = (key >> HALF_BITS).astype(jnp.int16)
        return carry

    lax.fori_loop(0, nch, score_chunk, 0)

    def count(pred):
        def body(c, acc):
            for j in range(sub):
                hit = jnp.where(pred(keys_scr[c, j * LANES:(j + 1) * LANES, :]), 1, 0)
                acc = acc + hit.reshape(LANES // SUBLANES, SUBLANES, tq).sum(axis=0)
            return acc
        acc = lax.fori_loop(0, nch, body, jnp.zeros((SUBLANES, tq), jnp.int32))
        return jnp.sum(acc, axis=0, keepdims=True)

    def half_tile(x):
        return jnp.broadcast_to(x, (PACKED_ROWS, tq)).astype(jnp.int16)

    def count16(scr, pred):
        def body(c, accs):
            accs = list(accs)
            for j in range(ck // PACKED_ROWS):
                hit = jnp.where(pred(scr[c, j * PACKED_ROWS:(j + 1) * PACKED_ROWS, :]), jnp.int16(1), jnp.int16(0))
                accs[j % len(accs)] = accs[j % len(accs)] + hit
            return tuple(accs)
        accs = lax.fori_loop(0, nch, body, tuple(jnp.zeros((PACKED_ROWS, tq), jnp.int16) for _ in range(4)))
        total = (accs[0] + accs[1]) + (accs[2] + accs[3])
        return jnp.sum(total.astype(jnp.int32), axis=0, keepdims=True)

    def kth_half(scr, k_need):
        def body(it, u):
            cand = u | lax.shift_left(jnp.int32(1), HALF_BITS - 1 - it)
            tile = half_tile(cand + HALF_MIN)
            return jnp.where(count16(scr, lambda t: t >= tile) >= k_need, cand, u)
        return lax.fori_loop(0, HALF_BITS, body, jnp.zeros((1, tq), jnp.int32))

    hi = kth_half(hi_scr, k_top) + HALF_MIN
    hi_tile = half_tile(hi)
    k_low = k_top - count16(hi_scr, lambda t: t > hi_tile)

    def low_chunk(c, carry):
        key = keys_scr[c]
        low = (key & HALF_MASK) + HALF_MIN
        lo_scr[c] = jnp.where((key >> HALF_BITS) == hi, low, HALF_MIN).astype(jnp.int16)
        return carry

    lax.fori_loop(0, nch, low_chunk, 0)
    thr = hi * (HALF_MASK + 1) + kth_half(lo_scr, k_low)
    need = k_top - count(lambda kk: kk > thr)
    n_tie = count(lambda kk: kk == thr)
    surplus = jnp.max(jnp.where((n_tie > need) & (thr > KEY_NEG_INF), 1, 0))

    @pl.when(surplus > 0)
    def _():
        ltri = _tri(lower=True)
        need_f = need.astype(F32)

        def tie_chunk(c, carry):
            for j in range(sub):
                kk = keys_scr[c, j * LANES:(j + 1) * LANES, :]
                eq = kk == thr
                eqf = jnp.where(eq, 1.0, 0.0)
                pc = _dot(ltri, eqf.astype(BF16)) + carry
                keys_scr[c, j * LANES:(j + 1) * LANES, :] = jnp.where(eq & (pc > need_f), KEY_NEG_INF, kk)
                carry = carry + jnp.sum(eqf, axis=0, keepdims=True)
            return carry

        lax.fori_loop(0, nch, tie_chunk, jnp.zeros((1, tq), F32))

    thr_sel = jnp.maximum(thr, KEY_NEG_INF + 1)

    qt = qt_ref[...] * HEAD_DIM ** -0.5
    row_head = lax.broadcasted_iota(jnp.int32, (ATT_WIDTH, tq), 0) // HEAD_DIM
    q_heads = [jnp.where(row_head == h, qt, 0.0).astype(BF16) for h in range(ATT_HEADS)]

    def att_chunk(c, st):
        ms, ls, accs = st
        k_c = k_ref[pl.ds(pl.multiple_of(c * ck, ck), ck), :]
        bias = jnp.where(keys_scr[c] >= thr_sel, 0.0, NEG_BIG)
        vt = vt_ref[c]
        new_m, new_l, new_acc = [], [], []
        for h in range(ATT_HEADS):
            s = _dot(k_c, q_heads[h]) + bias
            m_new = jnp.maximum(ms[h], jnp.max(s, axis=0, keepdims=True))
            alpha = jnp.exp(ms[h] - m_new)
            p = jnp.exp(s - m_new)
            new_m.append(m_new)
            new_l.append(ls[h] * alpha + jnp.sum(p, axis=0, keepdims=True))
            new_acc.append(accs[h] * alpha + _dot(vt[h * HEAD_DIM:(h + 1) * HEAD_DIM, :], p.astype(BF16)))
        return tuple(new_m), tuple(new_l), tuple(new_acc)

    init = (tuple(jnp.full((1, tq), NEG_BIG, F32) for _ in range(ATT_HEADS)),
            tuple(jnp.zeros((1, tq), F32) for _ in range(ATT_HEADS)),
            tuple(jnp.zeros((HEAD_DIM, tq), F32) for _ in range(ATT_HEADS)))
    _, ls, accs = lax.fori_loop(0, nch, att_chunk, init)
    out_t = jnp.concatenate([accs[h] / ls[h] for h in range(ATT_HEADS)], axis=0)
    o_ref[...] = out_t.T


def _prompt_attend(qit, qt, misct, kidx_bf, k_bf, vt, *, tq):
    nb, nchunks, _, ck = vt.shape
    t = nchunks * ck
    k_top = min(TOPK_MAX, t // 4)

    def tile_t(c):
        return pl.BlockSpec((None, c, tq), lambda b, i: (b, 0, i))

    def full(c):
        return pl.BlockSpec((None, t, c), lambda b, i: (b, 0, 0))

    return pl.pallas_call(
        functools.partial(_pattn_kernel, tq=tq, ck=ck, k_top=k_top),
        grid=(nb, t // tq),
        in_specs=[tile_t(IDX_HEADS * IDX_DIM), tile_t(ATT_WIDTH), tile_t(LANES),
                  full(IDX_DIM), full(ATT_WIDTH),
                  pl.BlockSpec((None, nchunks, ATT_WIDTH, ck), lambda b, i: (b, 0, 0, 0))],
        out_specs=pl.BlockSpec((None, tq, ATT_WIDTH), lambda b, i: (b, i, 0)),
        out_shape=jax.ShapeDtypeStruct((nb, t, ATT_WIDTH), F32),
        scratch_shapes=[pltpu.VMEM((nchunks, ck, tq), jnp.int32), pltpu.VMEM((nchunks, ck, tq), jnp.int16),
                        pltpu.VMEM((nchunks, ck, tq), jnp.int16)],
        compiler_params=_params(2),
        name="prompt_attend",
    )(qit, qt, misct, kidx_bf, k_bf, vt)


TOK_PAD = 8


def _sscore_kernel(pt_ref, qs_ref, misc_ref, knew_ref, *rest, pps, n_pages, t_new):
    page_refs = rest[:pps]
    keys_ref = rest[pps]
    j = pl.program_id(1)
    nsteps = n_pages // pps
    qs = qs_ref[...]
    wcol = misc_ref[:, WI_LANE:WI_LANE + IDX_HEADS] * IDX_HEADS ** -0.5

    def score_keys(kt):
        s = _dot(qs, kt.astype(BF16)) * IDX_DIM ** -0.5
        sc = jnp.zeros((TOK_PAD, kt.shape[1]), F32)
        for h in range(IDX_HEADS):
            sc = sc + jnp.maximum(s[h * TOK_PAD:(h + 1) * TOK_PAD], 0.0) * wcol[:, h:h + 1]
        return _score_key(sc)

    keys = score_keys(jnp.concatenate([r[...] for r in page_refs], axis=1))
    for p in range(pps):
        keys_ref[j * pps + p] = keys[:, p * PAGE_SIZE:(p + 1) * PAGE_SIZE]

    @pl.when(j == nsteps - 1)
    def _():
        row = lax.broadcasted_iota(jnp.int32, (TOK_PAD, LANES), 0)
        col = lax.broadcasted_iota(jnp.int32, (TOK_PAD, LANES), 1)
        keys_ref[n_pages] = jnp.where((col <= row) & (col < t_new), score_keys(knew_ref[...]), KEY_NEG_INF)


def _sample_scores(page_table, qs, misc_bm, kidx_new_t, cache_kidx_t, layer, *, pps, t_new):
    nb, n_pages = page_table.shape

    def per_b(shape):
        return pl.BlockSpec((None,) + shape, lambda b, j, pt: (b,) + (0,) * len(shape))

    def page(p):
        return pl.BlockSpec((None, None, IDX_DIM, PAGE_SIZE),
                            lambda b, j, pt: (pt[b, j * pps + p], layer, 0, 0))

    grid_spec = pltpu.PrefetchScalarGridSpec(
        num_scalar_prefetch=1,
        grid=(nb, n_pages // pps),
        in_specs=[per_b((IDX_HEADS * TOK_PAD, IDX_DIM)), per_b((TOK_PAD, LANES)), per_b((IDX_DIM, PAGE_SIZE))]
                 + [page(p) for p in range(pps)],
        out_specs=per_b((n_pages + 1, TOK_PAD, LANES)),
    )
    return pl.pallas_call(
        functools.partial(_sscore_kernel, pps=pps, n_pages=n_pages, t_new=t_new),
        grid_spec=grid_spec,
        out_shape=jax.ShapeDtypeStruct((nb, n_pages + 1, TOK_PAD, LANES), jnp.int32),
        compiler_params=_params(2),
        name="sample_scores",
    )(page_table, qs, misc_bm, kidx_new_t, *([cache_kidx_t] * pps))


def _ssel_kernel(keys_ref, bias_ref, *, nch, grp, k_top):
    rows = keys_ref.shape[1]

    def count(pred):
        def body(g, acc):
            for u in range(grp):
                acc = acc + jnp.where(pred(keys_ref[g * grp + u]), 1, 0)
            return acc
        acc = lax.fori_loop(0, nch // grp, body, jnp.zeros((rows, LANES), jnp.int32))
        return jnp.sum(acc, axis=1, keepdims=True)

    thr = _kth_largest(lambda cand: count(lambda kk: kk >= cand), (rows, 1), k_top)
    need = (k_top - count(lambda kk: kk > thr)).astype(F32)
    utri = _tri(lower=False)

    def tie_chunk(c, carry):
        kk = keys_ref[c]
        eq = kk == thr
        eqf = jnp.where(eq, 1.0, 0.0)
        pc = _dot(eqf.astype(BF16), utri) + carry
        sel = (kk > thr) | (eq & (pc <= need) & (thr > KEY_NEG_INF))
        bias_ref[c] = jnp.where(sel, 0.0, NEG_BIG)
        return carry + jnp.sum(eqf, axis=1, keepdims=True)

    lax.fori_loop(0, nch, tie_chunk, jnp.zeros((rows, 1), F32))


def _sample_select(keys, *, k_top):
    nch, rows, _ = keys.shape
    grp = 5 if nch % 5 == 0 else 1
    return pl.pallas_call(
        functools.partial(_ssel_kernel, nch=nch, grp=grp, k_top=k_top),
        grid=(1,),
        in_specs=[pl.BlockSpec(keys.shape, lambda i: (0, 0, 0))],
        out_specs=pl.BlockSpec(keys.shape, lambda i: (0, 0, 0)),
        out_shape=jax.ShapeDtypeStruct(keys.shape, F32),
        compiler_params=_params(1),
        name="sample_select",
    )(keys)


def _sattn_kernel(pt_ref, qbd_ref, bias_ref, knew_ref, vnew_ref, *rest, pps, n_pages):
    k_refs = rest[:pps]
    v_refs = rest[pps:2 * pps]
    o_ref = rest[2 * pps]
    m_scr, l_scr, acc_scr = rest[2 * pps + 1:]
    j = pl.program_id(1)
    nsteps = n_pages // pps
    rows = ATT_HEADS * TOK_PAD

    @pl.when(j == 0)
    def _():
        m_scr[...] = jnp.full((rows, 1), NEG_BIG, F32)
        l_scr[...] = jnp.zeros((rows, 1), F32)
        acc_scr[...] = jnp.zeros((rows, ATT_WIDTH), F32)

    qbd = qbd_ref[...]

    def page_t(ref):
        return ref[...].reshape(ATT_WIDTH, PAGE_SIZE).astype(BF16)

    def update(kt, vt, bias):
        s = _dot(qbd, kt) * HEAD_DIM ** -0.5 + jnp.concatenate([bias] * ATT_HEADS, axis=0)
        m = m_scr[...]
        m_new = jnp.maximum(m, jnp.max(s, axis=1, keepdims=True))
        alpha = jnp.exp(m - m_new)
        p = jnp.exp(s - m_new)
        l_scr[...] = l_scr[...] * alpha + jnp.sum(p, axis=1, keepdims=True)
        acc_scr[...] = acc_scr[...] * alpha + _dot_nt(p.astype(BF16), vt)
        m_scr[...] = m_new

    update(jnp.concatenate([page_t(r) for r in k_refs], axis=1), jnp.concatenate([page_t(r) for r in v_refs], axis=1),
           jnp.concatenate([bias_ref[j * pps + p] for p in range(pps)], axis=1))

    @pl.when(j == nsteps - 1)
    def _():
        update(knew_ref[...], vnew_ref[...], bias_ref[n_pages])
        full = acc_scr[...] / l_scr[...]
        lane_head = lax.broadcasted_iota(jnp.int32, (TOK_PAD, ATT_WIDTH), 1) // HEAD_DIM
        out = jnp.zeros((TOK_PAD, ATT_WIDTH), F32)
        for h in range(ATT_HEADS):
            out = out + jnp.where(lane_head == h, full[h * TOK_PAD:(h + 1) * TOK_PAD], 0.0)
        o_ref[...] = out


def _sample_attend(page_table, qbd, bias, kt_new, vt_new, cache_kt, cache_vt, layer, *, pps):
    nb, n_pages = page_table.shape
    rows = ATT_HEADS * TOK_PAD

    def per_b(shape):
        return pl.BlockSpec((None,) + shape, lambda b, j, pt: (b,) + (0,) * len(shape))

    def page(p):
        return pl.BlockSpec((None, None, ATT_HEADS, HEAD_DIM, PAGE_SIZE),
                            lambda b, j, pt: (pt[b, j * pps + p], layer, 0, 0, 0))

    grid_spec = pltpu.PrefetchScalarGridSpec(
        num_scalar_prefetch=1,
        grid=(nb, n_pages // pps),
        in_specs=[per_b((rows, ATT_WIDTH)), per_b((n_pages + 1, TOK_PAD, LANES)),
                  per_b((ATT_WIDTH, PAGE_SIZE)), per_b((ATT_WIDTH, PAGE_SIZE))]
                 + [page(p) for p in range(pps)] * 2,
        out_specs=per_b((TOK_PAD, ATT_WIDTH)),
        scratch_shapes=[pltpu.VMEM((rows, 1), F32), pltpu.VMEM((rows, 1), F32), pltpu.VMEM((rows, ATT_WIDTH), F32)],
    )
    return pl.pallas_call(
        functools.partial(_sattn_kernel, pps=pps, n_pages=n_pages),
        grid_spec=grid_spec,
        out_shape=jax.ShapeDtypeStruct((nb, TOK_PAD, ATT_WIDTH), F32),
        compiler_params=_params(2),
        name="sample_attend",
    )(page_table, qbd, bias, kt_new, vt_new, *([cache_kt] * pps), *([cache_vt] * pps))


def _post_kernel(x_ref, ys_ref, ya_ref, yc_ref, wo_ref, g1_ref, b1_ref, wu_ref, fcw_ref, fcb_ref, wd_ref,
                 g2_ref, b2_ref, pffn_ref, o_ref, nffn_ref, halo, ubuf, acc_ref, *, tt, rs, nt, cw):
    i = pl.program_id(1)
    h_rows = (FFN_CONV_W - 1) * rs
    hf = _round_up(h_rows, SUBLANES)

    @pl.when(i == 0)
    def _():
        halo[hf - h_rows:hf, :] = pffn_ref[...]

    mix = _dot(ys_ref[...].astype(BF16), wo_ref[0:SSD_INNER, :])
    mix = mix + _dot(ya_ref[...].astype(BF16), wo_ref[SSD_INNER:SSD_INNER + ATT_WIDTH, :])
    mix = mix + _dot(yc_ref[...].astype(BF16), wo_ref[SSD_INNER + ATT_WIDTH:, :])
    x1 = _layer_norm(ALPHA * x_ref[...] + mix, g1_ref[...], b1_ref[...])
    x1b = x1.astype(BF16)

    def up_proj(src, dst):
        ubuf[hf - h_rows:hf, dst:dst + cw] = halo[hf - h_rows:hf, src:src + cw]
        ubuf[hf:hf + tt, dst:dst + cw] = _dot(x1b, wu_ref[:, src:src + cw])
        halo[hf - h_rows:hf, src:src + cw] = ubuf[hf + tt - h_rows:hf + tt, dst:dst + cw]

    def conv(src, dst):
        acc = fcb_ref[:, src:src + cw] + fcw_ref[0:1, src:src + cw] * ubuf[hf - h_rows:hf - h_rows + tt, dst:dst + cw]
        for jj in range(1, FFN_CONV_W):
            off = hf - (FFN_CONV_W - 1 - jj) * rs
            acc = acc + fcw_ref[jj:jj + 1, src:src + cw] * ubuf[off:off + tt, dst:dst + cw]
        return acc

    for c in range(D_FF // cw):
        cv = c * cw
        cg = D_FF + c * cw
        up_proj(cv, 0)
        up_proj(cg, cw)
        f = (_silu(conv(cg, cw)) * conv(cv, 0)).astype(BF16)
        part = _dot(f, wd_ref[cv:cv + cw, :])
        if c == 0:
            acc_ref[...] = part
        else:
            acc_ref[...] += part

    o_ref[...] = _layer_norm(ALPHA * x1 + acc_ref[...], g2_ref[...], b2_ref[...])

    @pl.when(i == nt - 1)
    def _():
        nffn_ref[...] = halo[hf - h_rows:hf, :]


def _post(x, ys, ya, yc, wl, prev_ffn, *, tt, rs):
    nb, t, _ = x.shape
    nt = t // tt
    hf = _round_up((FFN_CONV_W - 1) * rs, SUBLANES)

    def tile(c):
        return pl.BlockSpec((None, tt, c), lambda b, i: (b, i, 0))

    def const(shape):
        return pl.BlockSpec(shape, lambda b, i: (0,) * len(shape), pipeline_mode=pl.Buffered(1))

    sspec = pl.BlockSpec((None, prev_ffn.shape[1], 2 * D_FF), lambda b, i: (b, 0, 0))
    cw = D_FF // 2
    return pl.pallas_call(
        functools.partial(_post_kernel, tt=tt, rs=rs, nt=nt, cw=cw),
        grid=(nb, nt),
        in_specs=[tile(D_MODEL), tile(SSD_INNER), tile(ATT_WIDTH), tile(CC_WIDTH),
                  const((D_MODEL, D_MODEL)), const((1, D_MODEL)), const((1, D_MODEL)),
                  const((D_MODEL, 2 * D_FF)), const((FFN_CONV_W, 2 * D_FF)), const((1, 2 * D_FF)),
                  const((D_FF, D_MODEL)), const((1, D_MODEL)), const((1, D_MODEL)), sspec],
        out_specs=[tile(D_MODEL), sspec],
        out_shape=[jax.ShapeDtypeStruct((nb, t, D_MODEL), F32), jax.ShapeDtypeStruct(prev_ffn.shape, F32)],
        scratch_shapes=[pltpu.VMEM((hf, 2 * D_FF), F32), pltpu.VMEM((hf + tt, 2 * cw), F32),
                        pltpu.VMEM((tt, D_MODEL), F32)],
        compiler_params=_params(2),
        name="post",
    )(x, ys, ya, yc, wl["w_out"], wl["ln1_g"], wl["ln1_b"], wl["ffn_w_up"], wl["ffn_conv_w"], wl["ffn_conv_b"],
      wl["ffn_w_down"], wl["ln2_g"], wl["ln2_b"], prev_ffn)


def _pack_w_in(w_in):
    z, xbc, dt, q, k, v, qi, ki, wi, glu = jnp.split(w_in, [int(c) for c in np.cumsum(IN_SIZES)[:-1]], axis=-1)
    d = w_in.shape[0]
    misc = jnp.concatenate([ki, wi, jnp.zeros((d, D_MODEL, LANES - IDX_DIM - IDX_HEADS), w_in.dtype)], axis=-1)
    dtp = jnp.concatenate([dt, jnp.zeros((d, D_MODEL, LANES - SSD_HEADS), w_in.dtype)], axis=-1)
    return jnp.concatenate([z, xbc, q, k, v, qi, glu, misc, dtp], axis=-1).astype(BF16)


def _layer_weights(p, l):
    pad_l = LANES - SSD_HEADS
    return {
        "w_in": p["w_in_packed"][l],
        "ssd_conv_w": p["ssd_conv_w"][l], "ssd_conv_b": p["ssd_conv_b"][l][None],
        "cc_conv_w": p["cc_conv_w"][l], "cc_conv_b": p["cc_conv_b"][l][None],
        "cc_ln_g": p["cc_ln_g"][l][None], "cc_ln_b": p["cc_ln_b"][l][None],
        "dt_bias": jnp.pad(p["ssd_dt_bias"][l], (0, pad_l))[None],
        "a_log": jnp.pad(p["ssd_a_log"][l], (0, pad_l))[None],
        "ssd_d": jnp.repeat(p["ssd_d"][l], HEAD_DIM)[None],
        "ssd_norm_g": p["ssd_norm_g"][l][None],
        "w_out": p["w_out_bf"][l], "ln1_g": p["ln1_g"][l][None], "ln1_b": p["ln1_b"][l][None],
        "ffn_w_up": p["ffn_w_up_bf"][l], "ffn_conv_w": p["ffn_conv_w"][l], "ffn_conv_b": p["ffn_conv_b"][l][None],
        "ffn_w_down": p["ffn_w_down_bf"][l], "ln2_g": p["ln2_g"][l][None], "ln2_b": p["ln2_b"][l][None],
    }


def _rope_tables(pos):
    half = HEAD_DIM // 2
    inv = ROPE_THETA ** (-jnp.arange(half, dtype=F32) / half)
    ang = pos.astype(F32)[:, None] * inv[None, :]
    cos = jnp.cos(ang)
    sin = jnp.sin(ang)
    cos_h = jnp.concatenate([cos, cos], axis=-1)
    sin_h = jnp.concatenate([-sin, sin], axis=-1)
    n = pos.shape[0]
    pad1 = jnp.ones((n, LANES - IDX_DIM), F32)
    pad0 = jnp.zeros((n, LANES - IDX_DIM), F32)
    return {"cosq": jnp.tile(cos_h, (1, ATT_HEADS)), "sinq": jnp.tile(sin_h, (1, ATT_HEADS)),
            "cosm": jnp.concatenate([cos_h, pad1], axis=-1), "sinm": jnp.concatenate([sin_h, pad0], axis=-1)}


def _prompt_layer(h, wl, tabs, *, tt_pre, tt_post, tq):
    nb, t, _ = h.shape
    zeros = lambda rows, c: jnp.zeros((nb, rows, c), F32)
    (z, xbc, _, k, v, _, _, kidx, dt, ycc, qt, qit, misct, k_bf, kidx_bf, vt, nssd, ncc) = _premix(
        h, wl, tabs, zeros(SSD_CONV_W - 1, SSD_CONV_DIM), zeros(CC_CONV_W - 1, CC_WIDTH), tt=tt_pre, rs=1)
    yssd, ssm = _ssd(xbc, dt, z, jnp.zeros((nb, SSD_HEADS, HEAD_DIM, SSD_STATE), F32), wl)
    yatt = _prompt_attend(qit, qt, misct, kidx_bf, k_bf, vt, tq=tq)
    h, nffn = _post(h, yssd, yatt, ycc, wl, zeros(FFN_CONV_W - 1, 2 * D_FF), tt=tt_post, rs=1)
    states = (k.reshape(nb, t, ATT_HEADS, HEAD_DIM), v.reshape(nb, t, ATT_HEADS, HEAD_DIM), kidx, ssm,
              nssd, ncc, nffn)
    return h, states


def _to_time_major(state):
    b, r, c = state.shape
    return state.transpose(1, 0, 2).reshape(1, r * b, c)


def _to_batch_major(rows, nb):
    _, n, c = rows.shape
    return rows.reshape(n // nb, nb, c).transpose(1, 0, 2)


def _pad_rows(a, rows):
    return jnp.pad(a, ((0, 0), (0, rows - a.shape[1]), (0, 0)))


def _sample_layer(g, wl, tabs, l, cache_k, cache_v, cache_kidx, page_table, ssm_prev, ssd_prev, cc_prev,
                  ffn_prev, *, nb, t_new, pps):
    (z, xbc, q, k, v, qi, misc, kidx, dt, ycc, _, _, _, _, _, _, nssd, ncc) = _premix(
        g, wl, tabs, _to_time_major(ssd_prev), _to_time_major(cc_prev), tt=nb * t_new, rs=nb)
    bm = lambda a: _to_batch_major(a, nb)
    yssd, ssm = _ssd(_pad_rows(bm(xbc), SSD_CHUNK), _pad_rows(bm(dt), SSD_CHUNK), _pad_rows(bm(z), SSD_CHUNK),
                     ssm_prev, wl, t_valid=t_new)
    yssd = _to_time_major(yssd[:, :t_new])
    qi_bm = _pad_rows(bm(qi), TOK_PAD).reshape(nb, TOK_PAD, IDX_HEADS, IDX_DIM)
    qs = qi_bm.transpose(0, 2, 1, 3).reshape(nb, IDX_HEADS * TOK_PAD, IDX_DIM)
    def new_t(a):
        return _pad_rows(a, PAGE_SIZE).transpose(0, 2, 1)

    keys = _sample_scores(page_table, qs, _pad_rows(bm(misc), TOK_PAD), new_t(bm(kidx)),
                          cache_kidx, l, pps=pps, t_new=t_new)
    n_chunks = keys.shape[1]
    n_keys = (n_chunks - 1) * PAGE_SIZE + t_new
    key_rows = keys[:, :, :t_new].transpose(1, 0, 2, 3).reshape(n_chunks, nb * t_new, LANES)
    bias_rows = _sample_select(key_rows, k_top=min(TOPK_MAX, n_keys // 4))
    bias = bias_rows.reshape(n_chunks, nb, t_new, LANES).transpose(1, 0, 2, 3)
    bias = jnp.pad(bias, ((0, 0), (0, 0), (0, TOK_PAD - t_new), (0, 0)))
    q_bm = _pad_rows(bm(q), TOK_PAD).reshape(nb, TOK_PAD, ATT_HEADS, HEAD_DIM)
    eye = jnp.eye(ATT_HEADS, dtype=q_bm.dtype)
    qbd = jnp.einsum("bthd,hg->bhtgd", q_bm, eye).reshape(nb, ATT_HEADS * TOK_PAD, ATT_WIDTH)
    k_bm = bm(k)
    v_bm = bm(v)
    yatt = _sample_attend(page_table, qbd, bias, new_t(k_bm).astype(BF16), new_t(v_bm).astype(BF16),
                          cache_k, cache_v, l, pps=pps)
    yatt = _to_time_major(yatt[:, :t_new])
    g, nffn = _post(g, yssd, yatt, ycc, wl, _to_time_major(ffn_prev), tt=nb * t_new, rs=nb)
    states = (k_bm.reshape(nb, t_new, ATT_HEADS, HEAD_DIM), v_bm.reshape(nb, t_new, ATT_HEADS, HEAD_DIM),
              bm(kidx), ssm, bm(nssd), bm(ncc), bm(nffn))
    return g, states


def kernel(x_prompt, x_sample, cache_k, cache_v, cache_kidx, page_table, state_ssm, state_ssd_conv, state_cc_conv, state_ffn_conv, ln0_g, ln0_b, w_in, ssd_conv_w, ssd_conv_b, ssd_dt_bias, ssd_a_log, ssd_d, ssd_norm_g, cc_conv_w, cc_conv_b, cc_ln_g, cc_ln_b, w_out, ln1_g, ln1_b, ffn_w_up, ffn_conv_w, ffn_conv_b, ffn_w_down, ln2_g, ln2_b):
    depth = w_in.shape[0]
    p = {"w_in_packed": _pack_w_in(w_in), "ssd_conv_w": ssd_conv_w, "ssd_conv_b": ssd_conv_b,
         "ssd_dt_bias": ssd_dt_bias, "ssd_a_log": ssd_a_log, "ssd_d": ssd_d, "ssd_norm_g": ssd_norm_g,
         "cc_conv_w": cc_conv_w, "cc_conv_b": cc_conv_b, "cc_ln_g": cc_ln_g, "cc_ln_b": cc_ln_b,
         "w_out_bf": w_out.astype(BF16), "ln1_g": ln1_g, "ln1_b": ln1_b,
         "ffn_w_up_bf": ffn_w_up.astype(BF16), "ffn_conv_w": ffn_conv_w, "ffn_conv_b": ffn_conv_b,
         "ffn_w_down_bf": ffn_w_down.astype(BF16), "ln2_g": ln2_g, "ln2_b": ln2_b}
    layers = [_layer_weights(p, l) for l in range(depth)]

    bp, sp, _ = x_prompt.shape
    tt_pre = min(512, sp)
    h = _ln_rows(x_prompt.reshape(bp * sp, D_MODEL), ln0_g, ln0_b, tt_pre).reshape(bp, sp, D_MODEL)
    tabs_p = _rope_tables(jnp.arange(sp))
    p_states = []
    for l in range(depth):
        h, st = _prompt_layer(h, layers[l], tabs_p, tt_pre=tt_pre, tt_post=min(512, sp), tq=min(512, sp))
        p_states.append(st)

    nb, ts, _ = x_sample.shape
    n_pages = page_table.shape[1]
    pps = math.gcd(32, n_pages)
    cache_k = cache_k.transpose(0, 1, 3, 4, 2)
    cache_v = cache_v.transpose(0, 1, 3, 4, 2)
    cache_kidx = cache_kidx.transpose(0, 1, 3, 2)
    g = _ln_rows(x_sample.transpose(1, 0, 2).reshape(ts * nb, D_MODEL), ln0_g, ln0_b, ts * nb)
    g = g.reshape(1, ts * nb, D_MODEL)
    tabs_s = _rope_tables(PAST_LEN + jnp.arange(ts * nb) // nb)
    s_states = []
    for l in range(depth):
        g, st = _sample_layer(g, layers[l], tabs_s, l, cache_k, cache_v, cache_kidx, page_table,
                              state_ssm[:, l], state_ssd_conv[:, l], state_cc_conv[:, l], state_ffn_conv[:, l],
                              nb=nb, t_new=ts, pps=pps)
        s_states.append(st)
    y_sample = _to_batch_major(g, nb)

    p_out = [jnp.stack(a, axis=1) for a in zip(*p_states)]
    s_out = [jnp.stack(a, axis=1) for a in zip(*s_states)]
    return (h, y_sample, *p_out, *s_out)
```
